```python
import math
import jax, jax.numpy as jnp
from jax import lax
import numpy as np

D_MODEL = 1024
BATCH = 2
SEQ = 8192
DEPTH = 1
DEC_BATCH = 32
DEC_SEQ = 64
PAST_LEN = 4096

CHUNK = 64
D_MIX = D_MODEL
RW_WIDTH = D_MIX // 2
RW_HD = 64
RW_HEADS = RW_WIDTH // RW_HD
RW_W_RANK = 64
RW_A_RANK = 64
RW_G_RANK = 128
RW_COLS = 3 * RW_WIDTH + RW_W_RANK + RW_A_RANK + RW_G_RANK
RW_GN_EPS = 64e-5
DA_WIDTH = D_MIX - RW_WIDTH
DA_HEADS = 4
DA_HD = 64
DA_VD = 2 * DA_HD
DA_COLS = 3 * DA_WIDTH
DA_SCALE = 1.0 / math.sqrt(DA_HD)
ROT_DIM = DA_HD // 4
ROPE_THETA = 500000.0
Q_BLOCK = 128
IN_COLS = RW_COLS + DA_COLS
PEER_HEADS = 8
PEER_NKEYS = 128
PEER_N = PEER_NKEYS * PEER_NKEYS
PEER_DK = 256
PEER_HALF = PEER_DK // 2
PEER_TOPK = 16
PEER_BLOCK = 128
LN_EPS = 1e-5
ALPHA = (2.0 * DEPTH) ** 0.25
BETA = (8.0 * DEPTH) ** -0.25

kernel_name = 'hymba_rwkv7_diffattn_peer_stream_step'


def _layer_norm(x, g, b):
    xf = x.astype(jnp.float32)
    mu = jnp.mean(xf, -1, keepdims=True)
    xc = xf - mu
    var = jnp.mean(xc * xc, -1, keepdims=True)
    y = xc * lax.rsqrt(var + LN_EPS) * g.astype(jnp.float32) + b.astype(jnp.float32)
    return y.astype(x.dtype)


def _rope_partial(x, pos):
    half = ROT_DIM // 2
    inv = jnp.power(ROPE_THETA, -jnp.arange(half, dtype=jnp.float32) * (2.0 / ROT_DIM))
    ang = pos.astype(jnp.float32)[:, None] * inv[None, :]
    cos = jnp.cos(ang)[None, :, None, None, :]
    sin = jnp.sin(ang)[None, :, None, None, :]
    xr = x[..., :ROT_DIM].astype(jnp.float32)
    x1, x2 = xr[..., :half], xr[..., half:]
    rot = jnp.concatenate([x1 * cos - x2 * sin, x2 * cos + x1 * sin], -1).astype(x.dtype)
    return jnp.concatenate([rot, x[..., ROT_DIM:]], -1)


def _wkv_scan(r, w, k, v, kk, b, s0):
    def step(S, inp):
        r_t, w_t, k_t, v_t, kk_t, b_t = inp
        sa = jnp.einsum('bhvk,bhk->bhv', S, kk_t)
        S = S * w_t[:, :, None, :] - sa[..., None] * b_t[:, :, None, :] + v_t[..., None] * k_t[:, :, None, :]
        y = jnp.einsum('bhvk,bhk->bhv', S, r_t)
        return S, y
    xs = tuple(jnp.moveaxis(t, 1, 0) for t in (r, w, k, v, kk, b))
    s_fin, ys = lax.scan(step, s0, xs)
    return jnp.moveaxis(ys, 0, 1), s_fin


def _rwkv7_group(p, shift_prev, s0, mu, w0, w2, a0, a2, g2, k_k, k_a, r_k, lnx_g, lnx_b):
    B, T, _ = p.shape
    f32 = jnp.float32
    pf = p.astype(f32)
    prev = jnp.concatenate([shift_prev.astype(f32), pf[:, :-1]], axis=1)
    pm = pf + (prev - pf) * mu.astype(f32)
    W = RW_WIDTH
    o1 = 3 * W
    o2 = o1 + RW_W_RANK
    o3 = o2 + RW_A_RANK
    r, k, v = pm[..., :W], pm[..., W:2 * W], pm[..., 2 * W:o1]
    wl, al, gl = pm[..., o1:o2], pm[..., o2:o3], pm[..., o3:]
    w = -jax.nn.softplus(-(w0.astype(f32) + jnp.tanh(wl) @ w2.astype(f32))) - 0.5
    decay = jnp.exp(-jnp.exp(w))
    a = jax.nn.sigmoid(a0.astype(f32) + al @ a2.astype(f32))
    g = jax.nn.sigmoid(gl) @ g2.astype(f32)
    heads = lambda t: t.reshape(B, T, RW_HEADS, RW_HD)
    kk = heads(k * k_k.astype(f32))
    kk = kk / jnp.maximum(jnp.sqrt(jnp.sum(kk * kk, -1, keepdims=True)), 1e-12)
    k = k * (1.0 + (a - 1.0) * k_a.astype(f32))
    rh, kh, vh, ah, dh = heads(r), heads(k), heads(v), heads(a), heads(decay)
    y, s_fin = _wkv_scan(rh, dh, kh, vh, kk, kk * ah, s0.astype(f32))
    ym = jnp.mean(y, -1, keepdims=True)
    yc = y - ym
    yv = jnp.mean(yc * yc, -1, keepdims=True)
    yn = (yc * lax.rsqrt(yv + RW_GN_EPS)).reshape(B, T, W) * lnx_g.astype(f32) + lnx_b.astype(f32)
    bonus = (jnp.sum(rh * kh * r_k.astype(f32), -1, keepdims=True) * vh).reshape(B, T, W)
    out = (yn + bonus) * g
    return out.astype(p.dtype), p[:, -1:], s_fin.astype(s0.dtype)


def _diff_maps(qb, k, v, lam, mask):
    s = jnp.einsum('bqhcd,bkhcd->bhcqk', qb.astype(jnp.float32), k.astype(jnp.float32)) * DA_SCALE
    if mask is not None:
        s = jnp.where(mask, s, -jnp.inf)
    pm = jax.nn.softmax(s, axis=-1)
    att = pm[:, :, 0] - lam * pm[:, :, 1]
    return jnp.einsum('bhqk,bkhe->bqhe', att, v.astype(jnp.float32))


def _diff_attn_prompt(q, k, v, lam):
    B, T = q.shape[0], q.shape[1]
    nb = T // Q_BLOCK
    kchunk = jnp.arange(T) // CHUNK
    def blk(i):
        start = i * Q_BLOCK
        qb = lax.dynamic_slice_in_dim(q, start, Q_BLOCK, axis=1)
        qchunk = (start + jnp.arange(Q_BLOCK)) // CHUNK
        mask = (kchunk[None, :] <= qchunk[:, None])[None, None, None]
        return _diff_maps(qb, k, v, lam, mask)
    o = lax.map(blk, jnp.arange(nb))
    return jnp.moveaxis(o, 0, 1).reshape(B, T, DA_HEADS, DA_VD)


def _mixer(h, pos, shift_prev, s0, ck, cv, w_in, shift_mu, rw_w0, rw_w2, rw_a0, rw_a2, rw_g2,
           rw_kk, rw_ka, rw_rk, rw_lnx_g, rw_lnx_b, da_lq1, da_lk1, da_lq2, da_lk2, da_subln_g,
           w_out, lam_init):
    B, T, _ = h.shape
    proj = h @ w_in
    p_rw, p_da = proj[..., :RW_COLS], proj[..., RW_COLS:]
    rw_out, new_shift, new_s = _rwkv7_group(p_rw, shift_prev, s0, shift_mu, rw_w0, rw_w2, rw_a0,
                                            rw_a2, rw_g2, rw_kk, rw_ka, rw_rk, rw_lnx_g, rw_lnx_b)
    q = p_da[..., :DA_WIDTH].reshape(B, T, DA_HEADS, 2, DA_HD)
    k = p_da[..., DA_WIDTH:2 * DA_WIDTH].reshape(B, T, DA_HEADS, 2, DA_HD)
    v = p_da[..., 2 * DA_WIDTH:].reshape(B, T, DA_HEADS, DA_VD)
    q = _rope_partial(q, pos)
    k = _rope_partial(k, pos)
    f32 = jnp.float32
    lam = (jnp.exp(jnp.sum(da_lq1.astype(f32) * da_lk1.astype(f32)))
           - jnp.exp(jnp.sum(da_lq2.astype(f32) * da_lk2.astype(f32))) + lam_init)
    if ck is None:
        o = _diff_attn_prompt(q, k, v, lam)
    else:
        P = ck.shape[1]
        k_all = jnp.concatenate([ck.reshape(B, P, DA_HEADS, 2, DA_HD).astype(k.dtype), k], axis=1)
        v_all = jnp.concatenate([cv.astype(v.dtype), v], axis=1)
        o = _diff_maps(q, k_all, v_all, lam, None)
    o = o * lax.rsqrt(jnp.mean(o * o, -1, keepdims=True) + LN_EPS) * da_subln_g.astype(f32) * (1.0 - lam_init)
    da_out = o.reshape(B, T, DA_WIDTH).astype(h.dtype)
    m = jnp.concatenate([rw_out, da_out], axis=-1) @ w_out
    return m, k.reshape(B, T, DA_HEADS, 2 * DA_HD), v, new_s, new_shift


def _peer(h, wq, subkeys, u_tab, v_tab):
    B, T, D = h.shape
    n = B * T
    nb = -(-n // PEER_BLOCK)
    xf = jnp.pad(h.reshape(n, D), ((0, nb * PEER_BLOCK - n), (0, 0))).reshape(nb, PEER_BLOCK, D)
    f32 = jnp.float32
    def blk(xb):
        q = (xb @ wq).reshape(PEER_BLOCK, PEER_HEADS, 2, PEER_HALF).astype(f32)
        s = jnp.einsum('thcd,hcnd->thcn', q, subkeys.astype(f32))
        s1, i1 = lax.top_k(s[:, :, 0], PEER_TOPK)
        s2, i2 = lax.top_k(s[:, :, 1], PEER_TOPK)
        cand = (s1[..., :, None] + s2[..., None, :]).reshape(PEER_BLOCK, PEER_HEADS, PEER_TOPK * PEER_TOPK)
        cidx = (i1[..., :, None] * PEER_NKEYS + i2[..., None, :]).reshape(PEER_BLOCK, PEER_HEADS, PEER_TOPK * PEER_TOPK)
        top, sel = lax.top_k(cand, PEER_TOPK)
        idx = jnp.take_along_axis(cidx, sel, axis=-1)
        gate = jax.nn.softmax(top, axis=-1)
        u = jnp.take(u_tab, idx, axis=0).astype(f32)
        act = jax.nn.gelu(jnp.einsum('thed,td->the', u, xb.astype(f32)), approximate=False)
        vv = jnp.take(v_tab, idx, axis=0).astype(f32)
        return jnp.einsum('the,thed->td', gate * act, vv).astype(xb.dtype)
    out = lax.map(blk, xf).reshape(nb * PEER_BLOCK, D)[:n]
    return out.reshape(B, T, D)


def setup_inputs(seed: int = 0) -> dict:
    key = jax.random.key(seed)
    ks = jax.random.split(key, 40)
    nrm = lambda i, shape, s: s * jax.random.normal(ks[i], shape, jnp.float32)
    L = DEPTH
    vscale = jnp.concatenate([
        jnp.ones((2 * RW_WIDTH,), jnp.float32), jnp.full((RW_WIDTH,), BETA, jnp.float32),
        jnp.ones((RW_W_RANK + RW_A_RANK + RW_G_RANK + 2 * DA_WIDTH,), jnp.float32),
        jnp.full((DA_WIDTH,), BETA, jnp.float32)])
    return {
        'x_prompt': nrm(0, (BATCH, SEQ, D_MODEL), 1.0),
        'x_sample': nrm(1, (DEC_BATCH, DEC_SEQ, D_MODEL), 1.0),
        'cache_k': nrm(2, (L, DEC_BATCH, PAST_LEN, DA_HEADS, 2 * DA_HD), 1.0),
        'cache_v': nrm(3, (L, DEC_BATCH, PAST_LEN, DA_HEADS, DA_VD), 1.0),
        'state_wkv': nrm(4, (L, DEC_BATCH, RW_HEADS, RW_HD, RW_HD), 0.1),
        'state_shift': nrm(5, (L, DEC_BATCH, 1, RW_COLS), 1.0),
        'ln_in_g': 1.0 + nrm(6, (D_MODEL,), 0.02),
        'ln_in_b': nrm(7, (D_MODEL,), 0.02),
        'w_in': nrm(8, (L, D_MODEL, IN_COLS), D_MODEL ** -0.5) * vscale,
        'shift_mu': jax.random.uniform(ks[9], (L, RW_COLS), jnp.float32),
        'rw_w0': jax.random.uniform(ks[10], (L, RW_WIDTH), jnp.float32, -5.0, 1.0),
        'rw_w2': nrm(11, (L, RW_W_RANK, RW_WIDTH), 0.1),
        'rw_a0': nrm(12, (L, RW_WIDTH), 0.1),
        'rw_a2': nrm(13, (L, RW_A_RANK, RW_WIDTH), RW_A_RANK ** -0.5),
        'rw_g2': nrm(14, (L, RW_G_RANK, RW_WIDTH), RW_G_RANK ** -0.5),
        'rw_kk': 0.85 + nrm(15, (L, RW_WIDTH), 0.05),
        'rw_ka': 1.0 + nrm(16, (L, RW_WIDTH), 0.05),
        'rw_rk': nrm(17, (L, RW_HEADS, RW_HD), 0.1),
        'rw_lnx_g': 1.0 + nrm(18, (L, RW_WIDTH), 0.02),
        'rw_lnx_b': nrm(19, (L, RW_WIDTH), 0.02),
        'da_lq1': nrm(20, (L, DA_HD), 0.1),
        'da_lk1': nrm(21, (L, DA_HD), 0.1),
        'da_lq2': nrm(22, (L, DA_HD), 0.1),
        'da_lk2': nrm(23, (L, DA_HD), 0.1),
        'da_subln_g': 1.0 + nrm(24, (L, DA_VD), 0.02),
        'w_out': nrm(25, (L, D_MIX, D_MODEL), BETA * D_MIX ** -0.5),
        'ln1_g': 1.0 + nrm(26, (L, D_MODEL), 0.02),
        'ln1_b': nrm(27, (L, D_MODEL), 0.02),
        'peer_wq': nrm(28, (L, D_MODEL, PEER_HEADS * PEER_DK), D_MODEL ** -0.5),
        'peer_subkeys': nrm(29, (L, PEER_HEADS, 2, PEER_NKEYS, PEER_HALF), PEER_HALF ** -0.5),
        'peer_u': nrm(30, (L, PEER_N, D_MODEL), D_MODEL ** -0.5),
        'peer_v': nrm(31, (L, PEER_N, D_MODEL), BETA),
        'ln2_g': 1.0 + nrm(32, (L, D_MODEL), 0.02),
        'ln2_b': nrm(33, (L, D_MODEL), 0.02),
    }


def reference(x_prompt, x_sample, cache_k, cache_v, state_wkv, state_shift, ln_in_g, ln_in_b,
              w_in, shift_mu, rw_w0, rw_w2, rw_a0, rw_a2, rw_g2, rw_kk, rw_ka, rw_rk, rw_lnx_g,
              rw_lnx_b, da_lq1, da_lk1, da_lq2, da_lk2, da_subln_g, w_out, ln1_g, ln1_b,
              peer_wq, peer_subkeys, peer_u, peer_v, ln2_g, ln2_b):
    pos_p = jnp.arange(x_prompt.shape[1])
    pos_s = PAST_LEN + jnp.arange(x_sample.shape[1])
    hp = _layer_norm(x_prompt, ln_in_g, ln_in_b)
    hs = _layer_norm(x_sample, ln_in_g, ln_in_b)
    bp = x_prompt.shape[0]
    kp_l, vp_l, sp_l, shp_l, ks_l, vs_l, ss_l, shs_l = [], [], [], [], [], [], [], []
    for l in range(DEPTH):
        lam_init = 0.8 - 0.6 * math.exp(-0.3 * l)

        def layer(h, pos, shift_prev, s0, ck, cv):
            m, nk, nv, ns, nsh = _mixer(h, pos, shift_prev, s0, ck, cv, w_in[l], shift_mu[l],
                                        rw_w0[l], rw_w2[l], rw_a0[l], rw_a2[l], rw_g2[l], rw_kk[l],
                                        rw_ka[l], rw_rk[l], rw_lnx_g[l], rw_lnx_b[l], da_lq1[l],
                                        da_lk1[l], da_lq2[l], da_lk2[l], da_subln_g[l], w_out[l],
                                        lam_init)
            h = _layer_norm(ALPHA * h + m, ln1_g[l], ln1_b[l])
            f = _peer(h, peer_wq[l], peer_subkeys[l], peer_u[l], peer_v[l])
            h = _layer_norm(ALPHA * h + f, ln2_g[l], ln2_b[l])
            return h, nk, nv, ns, nsh

        shift0 = jnp.zeros((bp, 1, RW_COLS), x_prompt.dtype)
        s_zero = jnp.zeros((bp, RW_HEADS, RW_HD, RW_HD), state_wkv.dtype)
        hp, kp, vp, sp, shp = layer(hp, pos_p, shift0, s_zero, None, None)
        hs, ksn, vsn, ssn, shsn = layer(hs, pos_s, state_shift[l], state_wkv[l], cache_k[l], cache_v[l])
        kp_l.append(kp); vp_l.append(vp); sp_l.append(sp); shp_l.append(shp)
        ks_l.append(ksn); vs_l.append(vsn); ss_l.append(ssn); shs_l.append(shsn)
    return (hp, hs, jnp.stack(kp_l), jnp.stack(vp_l), jnp.stack(sp_l), jnp.stack(shp_l),
            jnp.stack(ks_l), jnp.stack(vs_l), jnp.stack(ss_l), jnp.stack(shs_l))
```

```python
import functools
import math

import jax
import jax.numpy as jnp
from jax import lax
from jax.experimental import pallas as pl
from jax.experimental.pallas import tpu as pltpu

f32 = jnp.float32
bf16 = jnp.bfloat16

CHUNK = 64
RW_HD = 64
RW_W_RANK = 64
RW_A_RANK = 64
RW_G_RANK = 128
RW_GN_EPS = 64e-5
DA_HEADS = 4
DA_HD = 64
DA_VD = 2 * DA_HD
ROT_DIM = DA_HD // 4
ROPE_THETA = 500000.0
DA_SCALE = 1.0 / math.sqrt(DA_HD)
PEER_HEADS = 8
PEER_NKEYS = 128
PEER_HALF = 128
PEER_TOPK = 16
LN_EPS = 1e-5

LANES = 128
VMEM_LIMIT = 56 * 1024 * 1024


def _cparams(*sem):
    return pltpu.CompilerParams(dimension_semantics=sem, vmem_limit_bytes=VMEM_LIMIT)


def _layer_norm(x, g, b):
    mu = jnp.mean(x, -1, keepdims=True)
    xc = x - mu
    var = jnp.mean(xc * xc, -1, keepdims=True)
    return xc * lax.rsqrt(var + LN_EPS) * g + b


def _split3(x):
    hi = x.astype(bf16)
    r1 = x - hi.astype(f32)
    mid = r1.astype(bf16)
    lo = (r1 - mid.astype(f32)).astype(bf16)
    return hi, mid, lo


def _dot_exact_lhs(a01, x):
    a = a01.astype(bf16)
    hi, mid, lo = _split3(x)
    d = lambda y: jnp.dot(a, y, preferred_element_type=f32)
    return d(hi) + d(mid) + d(lo)


def _dot_exact_rhs(x, b01):
    b = b01.astype(bf16)
    hi, mid, lo = _split3(x)
    d = lambda y: jnp.dot(y, b, preferred_element_type=f32)
    return d(hi) + d(mid) + d(lo)


def _ln_proj_body(x_ref, g_ref, b_ref, w_ref, c_ref, s1_ref, s2_ref,
                  h_ref, prw_ref, k_ref, v_ref, qb_ref, kb_ref, vb_ref, *, apply_ln, rw_cols, da_w):
    x = x_ref[...]
    h = _layer_norm(x, g_ref[...], b_ref[...]) if apply_ln else x
    h_ref[...] = h
    p = jnp.dot(h.astype(bf16), w_ref[...], preferred_element_type=f32)
    prw_ref[...] = p[:, :rw_cols]
    q = p[:, rw_cols:rw_cols + da_w]
    k = p[:, rw_cols + da_w:rw_cols + 2 * da_w]
    v = p[:, rw_cols + 2 * da_w:]
    rep = da_w // LANES
    c = jnp.concatenate([c_ref[...]] * rep, axis=1)
    s1 = jnp.concatenate([s1_ref[...]] * rep, axis=1)
    s2 = jnp.concatenate([s2_ref[...]] * rep, axis=1)
    half = ROT_DIM // 2

    def rope(t):
        return t * c + pltpu.roll(t, da_w - half, 1) * s1 + pltpu.roll(t, half, 1) * s2

    qr = rope(q)
    kr = rope(k)
    k_ref[...] = kr
    v_ref[...] = v
    qb_ref[...] = (qr * DA_SCALE).astype(bf16)
    kb_ref[...] = kr.astype(bf16)
    vb_ref[...] = v.astype(bf16)


def _rope_tables(pos):
    half = ROT_DIM // 2
    inv = jnp.power(ROPE_THETA, -jnp.arange(half, dtype=f32) * (2.0 / ROT_DIM))
    ang = pos.astype(f32)[:, None] * inv[None, :]
    cos, sin = jnp.cos(ang), jnp.sin(ang)
    n = pos.shape[0]
    one = jnp.ones((n, DA_HD - ROT_DIM), f32)
    zero = jnp.zeros((n, DA_HD - ROT_DIM), f32)
    zh = jnp.zeros((n, half), f32)
    c = jnp.concatenate([cos, cos, one], 1)
    s1 = jnp.concatenate([-sin, zh, zero], 1)
    s2 = jnp.concatenate([zh, sin, zero], 1)
    two = lambda t: jnp.concatenate([t, t], 1)
    return two(c), two(s1), two(s2)


def _ln_proj(x, g, b, w_bf, pos, *, apply_ln, rw_cols, block):
    n, d = x.shape
    cols = w_bf.shape[1]
    da_w = (cols - rw_cols) // 3
    tm = block
    ntab = pos.shape[0] // tm
    c, s1, s2 = _rope_tables(pos)
    row = lambda i: (i, 0)
    tab = lambda i: (i % ntab, 0)
    full = lambda i: (0, 0)
    body = functools.partial(_ln_proj_body, apply_ln=apply_ln, rw_cols=rw_cols, da_w=da_w)
    return pl.pallas_call(
        body,
        grid=(n // tm,),
        in_specs=[pl.BlockSpec((tm, d), row), pl.BlockSpec((1, d), full), pl.BlockSpec((1, d), full),
                  pl.BlockSpec((d, cols), full),
                  pl.BlockSpec((tm, LANES), tab), pl.BlockSpec((tm, LANES), tab), pl.BlockSpec((tm, LANES), tab)],
        out_specs=[pl.BlockSpec((tm, d), row), pl.BlockSpec((tm, rw_cols), row),
                   pl.BlockSpec((tm, da_w), row), pl.BlockSpec((tm, da_w), row),
                   pl.BlockSpec((tm, da_w), row), pl.BlockSpec((tm, da_w), row), pl.BlockSpec((tm, da_w), row)],
        out_shape=[jax.ShapeDtypeStruct((n, d), f32), jax.ShapeDtypeStruct((n, rw_cols), f32),
                   jax.ShapeDtypeStruct((n, da_w), f32), jax.ShapeDtypeStruct((n, da_w), f32),
                   jax.ShapeDtypeStruct((n, da_w), bf16), jax.ShapeDtypeStruct((n, da_w), bf16),
                   jax.ShapeDtypeStruct((n, da_w), bf16)],
        compiler_params=_cparams("parallel"),
        name="ln_proj",
    )(x, g.reshape(1, d), b.reshape(1, d), w_bf, c, s1, s2)


def _heads(x, nh):
    return jnp.stack([x[:, h * RW_HD:(h + 1) * RW_HD] for h in range(nh)])


def _bmm(spec, a, b):
    return jnp.einsum(spec, a.astype(bf16), b.astype(bf16), preferred_element_type=f32)


def _rwkv_body(p_ref, shift_ref, s0_ref, mu_ref, w0_ref, w2_ref, a0_ref, a2_ref, g2_ref, kk_ref, ka_ref,
               rk_ref, lng_ref, lnb_ref,
               out_ref, sfin_ref,
               s_scr, prev_scr, r_scr, ld_scr, k_scr, v_scr, kk_scr, b_scr, y_scr, *, width, nh, tb):
    t_i = pl.program_id(1)
    nt = pl.num_programs(1)

    @pl.when(t_i == 0)
    def _():
        s_scr[...] = s0_ref[0]
        prev_scr[...] = shift_ref[0]

    p = p_ref[0]
    rows = lax.broadcasted_iota(jnp.int32, p.shape, 0)
    prev = jnp.where(rows == 0, prev_scr[...], pltpu.roll(p, 1, 0))
    prev_scr[...] = p[tb - 1:tb, :]
    pm = p + (prev - p) * mu_ref[...]
    W = width
    r = pm[:, :W]
    k = pm[:, W:2 * W]
    v = pm[:, 2 * W:3 * W]
    wa = pm[:, 3 * W:3 * W + RW_W_RANK + RW_A_RANK]
    gl = pm[:, 3 * W + RW_W_RANK + RW_A_RANK:]
    lw = jnp.dot(jnp.tanh(wa).astype(bf16), w2_ref[...], preferred_element_type=f32)
    la = jnp.dot(wa.astype(bf16), a2_ref[...], preferred_element_type=f32)
    w = -jax.nn.softplus(-(w0_ref[...] + lw)) - 0.5
    ld_scr[...] = -jnp.exp(w)
    a = jax.nn.sigmoid(a0_ref[...] + la)
    g = jnp.dot(jax.nn.sigmoid(gl).astype(bf16), g2_ref[...], preferred_element_type=f32)

    lane_h = lax.broadcasted_iota(jnp.int32, (W, W), 0) // RW_HD
    lane_h2 = lax.broadcasted_iota(jnp.int32, (W, W), 1) // RW_HD
    grp = (lane_h == lane_h2).astype(f32)

    kk = k * kk_ref[...]
    ss = _dot_exact_rhs(kk * kk, grp)
    kk = kk / jnp.maximum(jnp.sqrt(ss), 1e-12)
    k2 = k * (1.0 + (a - 1.0) * ka_ref[...])
    r_scr[...] = r
    k_scr[...] = k2
    v_scr[...] = v
    kk_scr[...] = kk
    b_scr[...] = kk * a
    bonus = _dot_exact_rhs(r * k2 * rk_ref[...], grp) * v

    L = CHUNK
    ri = lax.broadcasted_iota(jnp.int32, (L, L), 0)
    ci = lax.broadcasted_iota(jnp.int32, (L, L), 1)
    tri_incl = (ri >= ci).astype(f32)
    strict = (ri > ci)[None]
    incl = (ri >= ci)[None]
    eye = (ri == ci)[None]

    def chunk(c, carry):
        sl = pl.ds(pl.multiple_of(c * L, L), L)
        lw_c = ld_scr[sl, :]
        cin = _dot_exact_lhs(tri_incl, lw_c)
        cex = cin - lw_c
        ctot = cin[L - 1:L, :]
        e_in, e_ex, e_nin, e_rem = jnp.exp(cin), jnp.exp(cex), jnp.exp(-cin), jnp.exp(ctot - cin)
        kk_c, b_c, k_c, r_c, v_c = kk_scr[sl, :], b_scr[sl, :], k_scr[sl, :], r_scr[sl, :], v_scr[sl, :]
        al = _heads(kk_c * e_ex, nh)
        be = _heads(b_c * e_nin, nh)
        ka = _heads(k_c * e_nin, nh)
        rh = _heads(r_c * e_in, nh)
        bel = _heads(b_c * e_rem, nh)
        kal = _heads(k_c * e_rem, nh)
        vh = _heads(v_c, nh)
        wl = _heads(jnp.exp(ctot), nh)
        nt_ = 'hlk,hmk->hlm'
        nn_ = 'hlm,hmk->hlk'
        tn_ = 'hlk,hlv->hkv'
        mab = jnp.where(strict, _bmm(nt_, al, be), 0.0)
        mak = jnp.where(strict, _bmm(nt_, al, ka), 0.0)
        nrb = jnp.where(incl, _bmm(nt_, rh, be), 0.0)
        nrk = jnp.where(incl, _bmm(nt_, rh, ka), 0.0)
        n_ = -mab
        f_, q_ = n_, n_
        for _ in range(5):
            q_ = _bmm(nn_, q_, q_)
            f_ = f_ + q_ + _bmm(nn_, f_, q_)
        s0 = s_scr[...]
        x_ = _bmm(nt_, al, s0) + _bmm(nn_, mak, vh)
        pp = x_ + _bmm(nn_, f_, x_)
        y = _bmm(nt_, rh, s0) - _bmm(nn_, nrb, pp) + _bmm(nn_, nrk, vh)
        s_scr[...] = s0 * wl + _bmm(tn_, vh, kal) - _bmm(tn_, pp, bel)
        y_scr[sl, :] = jnp.concatenate([y[h] for h in range(nh)], axis=1)
        return carry

    lax.fori_loop(0, tb // L, chunk, 0)

    y = y_scr[...]
    inv_n = 1.0 / RW_HD
    ym = _dot_exact_rhs(y, grp) * inv_n
    yc = y - ym
    yv = _dot_exact_rhs(yc * yc, grp) * inv_n
    yn = yc * lax.rsqrt(yv + RW_GN_EPS) * lng_ref[...] + lnb_ref[...]
    out_ref[0] = (yn + bonus) * g

    @pl.when(t_i == nt - 1)
    def _():
        sfin_ref[0] = s_scr[...]


def _rwkv(p_rw, shift_prev, s0, mu, w0, w2, a0, a2, g2, k_k, k_a, r_k, lnx_g, lnx_b, *, block):
    bsz, t, cols = p_rw.shape
    nh = s0.shape[1]
    W = nh * RW_HD
    tb = block
    joint = RW_W_RANK + RW_A_RANK
    w2p = jnp.concatenate([w2, jnp.zeros((RW_A_RANK, W), w2.dtype)], 0).astype(bf16)
    a2p = jnp.concatenate([jnp.zeros((RW_W_RANK, W), a2.dtype), a2], 0).astype(bf16)
    vec = lambda x: x.reshape(1, -1)
    cst = lambda shape: pl.BlockSpec(shape, lambda b, i: (0,) * len(shape))
    body = functools.partial(_rwkv_body, width=W, nh=nh, tb=tb)
    big = lambda: pltpu.VMEM((tb, W), f32)
    return pl.pallas_call(
        body,
        grid=(bsz, t // tb),
        in_specs=[pl.BlockSpec((1, tb, cols), lambda b, i: (b, i, 0)),
                  pl.BlockSpec((1, 1, cols), lambda b, i: (b, 0, 0)),
                  pl.BlockSpec((1, nh, RW_HD, RW_HD), lambda b, i: (b, 0, 0, 0)),
                  cst((1, cols)), cst((1, W)), cst((joint, W)), cst((1, W)), cst((joint, W)),
                  cst((RW_G_RANK, W)), cst((1, W)), cst((1, W)), cst((1, W)), cst((1, W)), cst((1, W))],
        out_specs=[pl.BlockSpec((1, tb, W), lambda b, i: (b, i, 0)),
                   pl.BlockSpec((1, nh, RW_HD, RW_HD), lambda b, i: (b, 0, 0, 0))],
        out_shape=[jax.ShapeDtypeStruct((bsz, t, W), f32),
                   jax.ShapeDtypeStruct((bsz, nh, RW_HD, RW_HD), f32)],
        scratch_shapes=[pltpu.VMEM((nh, RW_HD, RW_HD), f32), pltpu.VMEM((1, cols), f32),
                        big(), big(), big(), big(), big(), big(), big()],
        compiler_params=_cparams("arbitrary", "arbitrary"),
        name="rwkv7",
    )(p_rw, shift_prev, s0, vec(mu), vec(w0), w2p, vec(a0), a2p, g2.astype(bf16), vec(k_k), vec(k_a),
      vec(r_k), vec(lnx_g), vec(lnx_b))


def _lam(lq1_ref, lk1_ref, lq2_ref, lk2_ref, lam_init):
    e1 = jnp.exp(jnp.sum(lq1_ref[...] * lk1_ref[...], axis=-1, keepdims=True))
    e2 = jnp.exp(jnp.sum(lq2_ref[...] * lk2_ref[...], axis=-1, keepdims=True))
    return e1 - e2 + lam_init


def _sub_q(q):
    lane = lax.broadcasted_iota(jnp.int32, q.shape, 1)
    zero = jnp.zeros_like(q)
    return jnp.where(lane < DA_HD, q, zero), jnp.where(lane >= DA_HD, q, zero)


def _scores(qm, k):
    return lax.dot_general(qm, k, (((1,), (1,)), ((), ())), preferred_element_type=f32)


def _finish(o1, o2, lam, g, lam_init):
    o = o1 - lam * o2
    return o * lax.rsqrt(jnp.mean(o * o, -1, keepdims=True) + LN_EPS) * g * (1.0 - lam_init)


def _attn_prompt_body(q_ref, k_ref, v_ref, lq1_ref, lk1_ref, lq2_ref, lk2_ref, g_ref, o_ref,
                      m_scr, l_scr, acc_scr, *, tq, lam_init):
    i = pl.program_id(2)
    q0, q1 = _sub_q(q_ref[...])
    qs = (q0, q1)
    m_scr[...] = jnp.full(m_scr.shape, -jnp.inf, f32)
    l_scr[...] = jnp.zeros(l_scr.shape, f32)
    acc_scr[...] = jnp.zeros(acc_scr.shape, f32)

    def update(kb, vb, mask):
        for c in range(2):
            s = _scores(qs[c], kb)
            if mask is not None:
                s = jnp.where(mask, s, -jnp.inf)
            m_old = m_scr[c]
            m_new = jnp.maximum(m_old, jnp.max(s, -1, keepdims=True))
            alpha = jnp.exp(m_old - m_new)
            pexp = jnp.exp(s - m_new)
            l_scr[c] = alpha * l_scr[c] + jnp.sum(pexp, -1, keepdims=True)
            acc_scr[c] = alpha * acc_scr[c] + jnp.dot(pexp.astype(bf16), vb, preferred_element_type=f32)
            m_scr[c] = m_new

    def blk(j, carry):
        sl = pl.ds(pl.multiple_of(j * tq, tq), tq)
        update(k_ref[sl, :], v_ref[sl, :], None)
        return carry

    lax.fori_loop(0, i, blk, 0)
    sl = pl.ds(pl.multiple_of(i * tq, tq), tq)
    rq = lax.broadcasted_iota(jnp.int32, (tq, tq), 0) // CHUNK
    ck = lax.broadcasted_iota(jnp.int32, (tq, tq), 1) // CHUNK
    update(k_ref[sl, :], v_ref[sl, :], ck <= rq)
    lam = _lam(lq1_ref, lk1_ref, lq2_ref, lk2_ref, lam_init)
    o_ref[...] = _finish(acc_scr[0] / l_scr[0], acc_scr[1] / l_scr[1], lam, g_ref[...], lam_init)


def _attn_prompt(qb, kb, vb, lq1, lk1, lq2, lk2, g, *, bsz, lam_init, block):
    n, da_w = qb.shape
    t = n // bsz
    nhead = da_w // DA_VD
    tq = block
    nq = t // tq
    vec = lambda x: x.reshape(1, -1)
    cst = lambda w: pl.BlockSpec((1, w), lambda b, h, i: (0, 0))
    body = functools.partial(_attn_prompt_body, tq=tq, lam_init=lam_init)
    return pl.pallas_call(
        body,
        grid=(bsz, nhead, nq),
        in_specs=[pl.BlockSpec((tq, DA_VD), lambda b, h, i: (b * nq + i, h)),
                  pl.BlockSpec((t, DA_VD), lambda b, h, i: (b, h)),
                  pl.BlockSpec((t, DA_VD), lambda b, h, i: (b, h)),
                  cst(DA_HD), cst(DA_HD), cst(DA_HD), cst(DA_HD), cst(DA_VD)],
        out_specs=pl.BlockSpec((tq, DA_VD), lambda b, h, i: (b * nq + i, h)),
        out_shape=jax.ShapeDtypeStruct((n, da_w), f32),
        scratch_shapes=[pltpu.VMEM((2, tq, 1), f32), pltpu.VMEM((2, tq, 1), f32),
                        pltpu.VMEM((2, tq, DA_VD), f32)],
        compiler_params=_cparams("parallel", "parallel", "arbitrary"),
        name="diff_attn_prompt",
    )(qb, kb, vb, vec(lq1), vec(lk1), vec(lq2), vec(lk2), vec(g))


def _attn_sample_body(q_ref, kn_ref, vn_ref, ck_ref, cv_ref, lq1_ref, lk1_ref, lq2_ref, lk2_ref, g_ref, o_ref,
                      *, lam_init):
    qs = _sub_q(q_ref[...])
    ck = ck_ref[0].astype(bf16)
    cv = cv_ref[0].astype(bf16)
    kn = kn_ref[...]
    vn = vn_ref[...]
    outs = []
    for c in range(2):
        s_c = _scores(qs[c], ck)
        s_n = _scores(qs[c], kn)
        m = jnp.maximum(jnp.max(s_c, -1, keepdims=True), jnp.max(s_n, -1, keepdims=True))
        p_c = jnp.exp(s_c - m)
        p_n = jnp.exp(s_n - m)
        l = jnp.sum(p_c, -1, keepdims=True) + jnp.sum(p_n, -1, keepdims=True)
        o = (jnp.dot(p_c.astype(bf16), cv, preferred_element_type=f32)
             + jnp.dot(p_n.astype(bf16), vn, preferred_element_type=f32))
        outs.append(o / l)
    lam = _lam(lq1_ref, lk1_ref, lq2_ref, lk2_ref, lam_init)
    o_ref[...] = _finish(outs[0], outs[1], lam, g_ref[...], lam_init)


def _attn_sample(qb, kb, vb, cache_k, cache_v, lq1, lk1, lq2, lk2, g, *, lam_init):
    n, da_w = qb.shape
    bsz, past = cache_k.shape[0], cache_k.shape[1]
    t = n // bsz
    nhead = da_w // DA_VD
    ck = cache_k.reshape(bsz, past, da_w)
    cv = cache_v.reshape(bsz, past, da_w)
    vec = lambda x: x.reshape(1, -1)
    cst = lambda w: pl.BlockSpec((1, w), lambda b, h: (0, 0))
    new = pl.BlockSpec((t, DA_VD), lambda b, h: (b, h))
    old = pl.BlockSpec((1, past, DA_VD), lambda b, h: (b, 0, h))
    body = functools.partial(_attn_sample_body, lam_init=lam_init)
    return pl.pallas_call(
        body,
        grid=(bsz, nhead),
        in_specs=[new, new, new, old, old, cst(DA_HD), cst(DA_HD), cst(DA_HD), cst(DA_HD), cst(DA_VD)],
        out_specs=new,
        out_shape=jax.ShapeDtypeStruct((n, da_w), f32),
        compiler_params=_cparams("parallel", "parallel"),
        name="diff_attn_sample",
    )(qb, kb, vb, ck, cv, vec(lq1), vec(lk1), vec(lq2), vec(lk2), vec(g))


def _out_ln_body(h_ref, rw_ref, da_ref, w1_ref, w2_ref, g_ref, b_ref, o_ref, *, alpha):
    m = (jnp.dot(rw_ref[...].astype(bf16), w1_ref[...], preferred_element_type=f32)
         + jnp.dot(da_ref[...].astype(bf16), w2_ref[...], preferred_element_type=f32))
    o_ref[...] = _layer_norm(alpha * h_ref[...] + m, g_ref[...], b_ref[...])


def _out_ln(h, rw_out, da_out, w_out_bf, g, b, *, alpha, block):
    n, d = h.shape
    w_rw = rw_out.shape[1]
    w_da = da_out.shape[1]
    tm = block
    row = lambda i: (i, 0)
    full = lambda i: (0, 0)
    return pl.pallas_call(
        functools.partial(_out_ln_body, alpha=alpha),
        grid=(n // tm,),
        in_specs=[pl.BlockSpec((tm, d), row), pl.BlockSpec((tm, w_rw), row), pl.BlockSpec((tm, w_da), row),
                  pl.BlockSpec((w_rw, d), full), pl.BlockSpec((w_da, d), full),
                  pl.BlockSpec((1, d), full), pl.BlockSpec((1, d), full)],
        out_specs=pl.BlockSpec((tm, d), row),
        out_shape=jax.ShapeDtypeStruct((n, d), f32),
        compiler_params=_cparams("parallel"),
        name="out_proj_ln1",
    )(h, rw_out, da_out, w_out_bf[:w_rw], w_out_bf[w_rw:], g.reshape(1, d), b.reshape(1, d))


def _pair_list():
    return [(i, j) for i in range(PEER_TOPK) for j in range(PEER_TOPK) if (i + 1) * (j + 1) <= PEER_TOPK]


def _top_values(x):
    out = []
    for _ in range(PEER_TOPK):
        m = jnp.max(x, axis=0, keepdims=True)
        out.append(m)
        x = jnp.where(x == m, -jnp.inf, x)
    return out


def _peer_body(h_ref, wq_ref, sk_ref, u_ref, vt_ref, g_ref, b_ref, o_ref,
               s1_scr, s2_scr, e1_scr, e2_scr, tau_scr, acc_scr, *, alpha, a_per_blk):
    e = pl.program_id(1)
    ne = pl.num_programs(1)
    tb = h_ref.shape[0]
    nk = PEER_NKEYS

    @pl.when(e == 0)
    def _():
        hb = h_ref[...].astype(bf16)
        q = jnp.dot(hb, wq_ref[...], preferred_element_type=f32).astype(bf16)
        for h in range(PEER_HEADS):
            tops = []
            for c in range(2):
                col = (h * 2 + c) * PEER_HALF
                s = lax.dot_general(sk_ref[h, c], q[:, col:col + PEER_HALF], (((1,), (1,)), ((), ())),
                                    preferred_element_type=f32)
                (s1_scr if c == 0 else s2_scr)[h] = s
                tops.append(_top_values(s))
            t1, t2 = tops
            cands = [t1[i] + t2[j] for i, j in _pair_list()]
            pad = [jnp.full_like(cands[0], -jnp.inf)] * (-len(cands) % 8)
            tau = _top_values(jnp.concatenate(cands + pad, axis=0))[PEER_TOPK - 1]
            top = t1[0] + t2[0]
            z = jnp.zeros_like(tau)
            for cnd in cands:
                z = z + jnp.where(cnd >= tau, jnp.exp(cnd - top), 0.0)
            tau_scr[h] = tau
            e1_scr[h] = jnp.exp(s1_scr[h] - t1[0]) / z
            e2_scr[h] = jnp.exp(s2_scr[h] - t2[0])
        acc_scr[...] = jnp.zeros(acc_scr.shape, f32)

    act = lax.dot_general(u_ref[...], h_ref[...].astype(bf16), (((1,), (1,)), ((), ())),
                          preferred_element_type=f32)
    gel = 0.5 * act * (1.0 + lax.erf(act * (1.0 / math.sqrt(2.0))))
    gates = []
    for al in range(a_per_blk):
        a = e * a_per_blk + al
        gsum = jnp.zeros((nk, tb), f32)
        for h in range(PEER_HEADS):
            s1row = s1_scr[h, pl.ds(a, 1), :]
            e1row = e1_scr[h, pl.ds(a, 1), :]
            sel = (s2_scr[h] + s1row) >= tau_scr[h]
            gsum = gsum + jnp.where(sel, e2_scr[h], 0.0) * e1row
        gates.append(gsum)
    wgt = (jnp.concatenate(gates, axis=0) * gel).astype(bf16)
    acc_scr[...] += jnp.dot(vt_ref[...], wgt, preferred_element_type=f32)

    @pl.when(e == ne - 1)
    def _():
        f = acc_scr[...].T
        o_ref[...] = _layer_norm(alpha * h_ref[...] + f, g_ref[...], b_ref[...])


def _peer(h, wq_bf, sk_bf, u_bf, vt_bf, g, b, *, alpha, block, a_per_blk):
    n, d = h.shape
    tb = block
    eb = a_per_blk * PEER_NKEYS
    n_exp = u_bf.shape[0]
    qcols = wq_bf.shape[1]
    body = functools.partial(_peer_body, alpha=alpha, a_per_blk=a_per_blk)
    keyed = lambda: pltpu.VMEM((PEER_HEADS, PEER_NKEYS, tb), f32)
    return pl.pallas_call(
        body,
        grid=(n // tb, n_exp // eb),
        in_specs=[pl.BlockSpec((tb, d), lambda i, e: (i, 0)),
                  pl.BlockSpec((d, qcols), lambda i, e: (0, 0)),
                  pl.BlockSpec((PEER_HEADS, 2, PEER_NKEYS, PEER_HALF), lambda i, e: (0, 0, 0, 0)),
                  pl.BlockSpec((eb, d), lambda i, e: (e, 0)),
                  pl.BlockSpec((d, eb), lambda i, e: (0, e)),
                  pl.BlockSpec((1, d), lambda i, e: (0, 0)), pl.BlockSpec((1, d), lambda i, e: (0, 0))],
        out_specs=pl.BlockSpec((tb, d), lambda i, e: (i, 0)),
        out_shape=jax.ShapeDtypeStruct((n, d), f32),
        scratch_shapes=[keyed(), keyed(), keyed(), keyed(), pltpu.VMEM((PEER_HEADS, 1, tb), f32),
                        pltpu.VMEM((d, tb), f32)],
        compiler_params=_cparams("parallel", "arbitrary"),
        name="peer_ln2",
    )(h, wq_bf, sk_bf, u_bf, vt_bf, g.reshape(1, d), b.reshape(1, d))


def _pick(n, pref):
    b = pref
    while n % b:
        b //= 2
    return b


def kernel(x_prompt, x_sample, cache_k, cache_v, state_wkv, state_shift, ln_in_g, ln_in_b, w_in, shift_mu, rw_w0, rw_w2, rw_a0, rw_a2, rw_g2, rw_kk, rw_ka, rw_rk, rw_lnx_g, rw_lnx_b, da_lq1, da_lk1, da_lq2, da_lk2, da_subln_g, w_out, ln1_g, ln1_b, peer_wq, peer_subkeys, peer_u, peer_v, ln2_g, ln2_b):
    depth = w_in.shape[0]
    alpha = (2.0 * depth) ** 0.25
    bp, tp, d = x_prompt.shape
    bs, ts, _ = x_sample.shape
    past = cache_k.shape[2]
    nh_rw = state_wkv.shape[2]
    rw_w = nh_rw * RW_HD
    rw_cols = 3 * rw_w + RW_W_RANK + RW_A_RANK + RW_G_RANK
    da_w = DA_HEADS * DA_VD

    hp = x_prompt.reshape(bp * tp, d)
    hs = x_sample.reshape(bs * ts, d)
    pos_p = jnp.arange(tp)
    blk_s = _pick(bs * ts, 256)
    pos_s = past + (jnp.arange(max(blk_s, ts)) % ts)
    outs = [[] for _ in range(8)]
    for l in range(depth):
        lam_init = 0.8 - 0.6 * math.exp(-0.3 * l)
        w_in_bf = w_in[l].astype(bf16)
        w_out_bf = w_out[l].astype(bf16)
        wq_bf = peer_wq[l].astype(bf16)
        sk_bf = peer_subkeys[l].astype(bf16)
        u_bf = peer_u[l].astype(bf16)
        vt_bf = peer_v[l].astype(bf16).T

        def layer(h, bsz, t, pos, shift_prev, s0, ck, cv, blk):
            h, p_rw, k_rot, v, qb, kb, vb = _ln_proj(h, ln_in_g, ln_in_b, w_in_bf, pos, apply_ln=(l == 0),
                                                     rw_cols=rw_cols, block=blk)
            p_rw3 = p_rw.reshape(bsz, t, rw_cols)
            rw_out, s_fin = _rwkv(p_rw3, shift_prev, s0, shift_mu[l], rw_w0[l], rw_w2[l], rw_a0[l], rw_a2[l],
                                  rw_g2[l], rw_kk[l], rw_ka[l], rw_rk[l].reshape(-1), rw_lnx_g[l], rw_lnx_b[l],
                                  block=_pick(t, 256))
            if ck is None:
                da_out = _attn_prompt(qb, kb, vb, da_lq1[l], da_lk1[l], da_lq2[l], da_lk2[l], da_subln_g[l],
                                      bsz=bsz, lam_init=lam_init, block=_pick(t, 256))
            else:
                da_out = _attn_sample(qb, kb, vb, ck, cv, da_lq1[l], da_lk1[l], da_lq2[l], da_lk2[l],
                                      da_subln_g[l], lam_init=lam_init)
            h1 = _out_ln(h, rw_out.reshape(bsz * t, rw_w), da_out, w_out_bf, ln1_g[l], ln1_b[l],
                         alpha=alpha, block=blk)
            h2 = _peer(h1, wq_bf, sk_bf, u_bf, vt_bf, ln2_g[l], ln2_b[l], alpha=alpha,
                       block=_pick(bsz * t, 512), a_per_blk=8)
            return (h2, k_rot.reshape(bsz, t, DA_HEADS, DA_VD), v.reshape(bsz, t, DA_HEADS, DA_VD), s_fin,
                    p_rw3[:, t - 1:, :])

        shift0 = jnp.zeros((bp, 1, rw_cols), x_prompt.dtype)
        s_zero = jnp.zeros((bp, nh_rw, RW_HD, RW_HD), state_wkv.dtype)
        hp, kp, vp, sp, shp = layer(hp, bp, tp, pos_p, shift0, s_zero, None, None, _pick(bp * tp, 256))
        hs, ksn, vsn, ssn, shsn = layer(hs, bs, ts, pos_s, state_shift[l], state_wkv[l], cache_k[l], cache_v[l],
                                        blk_s)
        for lst, val in zip(outs, (kp, vp, sp, shp, ksn, vsn, ssn, shsn)):
            lst.append(val)
    st = [jnp.stack(o) for o in outs]
    return (hp.reshape(bp, tp, d), hs.reshape(bs, ts, d), *st)
```

```python
import functools
import math

import jax
import jax.numpy as jnp
from jax import lax
from jax.experimental import pallas as pl
from jax.experimental.pallas import tpu as pltpu

f32 = jnp.float32
bf16 = jnp.bfloat16

CHUNK = 64
RW_HD = 64
RW_W_RANK = 64
RW_A_RANK = 64
RW_G_RANK = 128
RW_GN_EPS = 64e-5
DA_HEADS = 4
DA_HD = 64
DA_VD = 2 * DA_HD
ROT_DIM = DA_HD // 4
ROPE_THETA = 500000.0
DA_SCALE = 1.0 / math.sqrt(DA_HD)
LOG2E = math.log2(math.e)
PEER_HEADS = 8
PEER_NKEYS = 128
PEER_HALF = 128
PEER_TOPK = 16
LN_EPS = 1e-5

LANES = 128
VMEM_LIMIT = 56 * 1024 * 1024


def _cparams(*sem):
    return pltpu.CompilerParams(dimension_semantics=sem, vmem_limit_bytes=VMEM_LIMIT)


def _layer_norm(x, g, b):
    mu = jnp.mean(x, -1, keepdims=True)
    xc = x - mu
    var = jnp.mean(xc * xc, -1, keepdims=True)
    return xc * lax.rsqrt(var + LN_EPS) * g + b


def _split3(x):
    hi = x.astype(bf16)
    r1 = x - hi.astype(f32)
    mid = r1.astype(bf16)
    lo = (r1 - mid.astype(f32)).astype(bf16)
    return hi, mid, lo


def _dot_exact_lhs(a01, x):
    a = a01.astype(bf16)
    hi, mid, lo = _split3(x)
    d = lambda y: jnp.dot(a, y, preferred_element_type=f32)
    return d(hi) + d(mid) + d(lo)


def _dot_exact_rhs(x, b01):
    b = b01.astype(bf16)
    hi, mid, lo = _split3(x)
    d = lambda y: jnp.dot(y, b, preferred_element_type=f32)
    return d(hi) + d(mid) + d(lo)


def _ln_proj_body(x_ref, g_ref, b_ref, w_ref, c_ref, s1_ref, s2_ref,
                  h_ref, prw_ref, k_ref, v_ref, qb_ref, kb_ref, vb_ref, *, apply_ln, rw_cols, da_w, feature_major):
    x = x_ref[...]
    h = _layer_norm(x, g_ref[...], b_ref[...]) if apply_ln else x
    h_ref[...] = h
    p = jnp.dot(h.astype(bf16), w_ref[...], preferred_element_type=f32)
    prw_ref[...] = p[:, :rw_cols]
    q = p[:, rw_cols:rw_cols + da_w]
    k = p[:, rw_cols + da_w:rw_cols + 2 * da_w]
    v = p[:, rw_cols + 2 * da_w:]
    rep = da_w // LANES
    c = jnp.concatenate([c_ref[...]] * rep, axis=1)
    s1 = jnp.concatenate([s1_ref[...]] * rep, axis=1)
    s2 = jnp.concatenate([s2_ref[...]] * rep, axis=1)
    half = ROT_DIM // 2

    def rope(t):
        return t * c + pltpu.roll(t, da_w - half, 1) * s1 + pltpu.roll(t, half, 1) * s2

    qr = rope(q)
    kr = rope(k)
    k_ref[...] = kr
    v_ref[...] = v
    kb_ref[...] = kr.astype(bf16)
    if feature_major:
        qb_ref[...] = (qr * (DA_SCALE * LOG2E)).T.astype(bf16)
        vb_ref[...] = v.T.astype(bf16)
    else:
        qb_ref[...] = (qr * DA_SCALE).astype(bf16)
        vb_ref[...] = v.astype(bf16)


def _rope_tables(pos):
    half = ROT_DIM // 2
    inv = jnp.power(ROPE_THETA, -jnp.arange(half, dtype=f32) * (2.0 / ROT_DIM))
    ang = pos.astype(f32)[:, None] * inv[None, :]
    cos, sin = jnp.cos(ang), jnp.sin(ang)
    n = pos.shape[0]
    one = jnp.ones((n, DA_HD - ROT_DIM), f32)
    zero = jnp.zeros((n, DA_HD - ROT_DIM), f32)
    zh = jnp.zeros((n, half), f32)
    c = jnp.concatenate([cos, cos, one], 1)
    s1 = jnp.concatenate([-sin, zh, zero], 1)
    s2 = jnp.concatenate([zh, sin, zero], 1)
    two = lambda t: jnp.concatenate([t, t], 1)
    return two(c), two(s1), two(s2)


def _ln_proj(x, g, b, w_bf, pos, *, apply_ln, rw_cols, block, feature_major):
    n, d = x.shape
    cols = w_bf.shape[1]
    da_w = (cols - rw_cols) // 3
    tm = block
    ntab = pos.shape[0] // tm
    c, s1, s2 = _rope_tables(pos)
    row = lambda i: (i, 0)
    tab = lambda i: (i % ntab, 0)
    full = lambda i: (0, 0)
    body = functools.partial(_ln_proj_body, apply_ln=apply_ln, rw_cols=rw_cols, da_w=da_w,
                             feature_major=feature_major)
    if feature_major:
        qv_spec = pl.BlockSpec((da_w, tm), lambda i: (0, i))
        qv_shape = jax.ShapeDtypeStruct((da_w, n), bf16)
    else:
        qv_spec = pl.BlockSpec((tm, da_w), row)
        qv_shape = jax.ShapeDtypeStruct((n, da_w), bf16)
    return pl.pallas_call(
        body,
        grid=(n // tm,),
        in_specs=[pl.BlockSpec((tm, d), row), pl.BlockSpec((1, d), full), pl.BlockSpec((1, d), full),
                  pl.BlockSpec((d, cols), full),
                  pl.BlockSpec((tm, LANES), tab), pl.BlockSpec((tm, LANES), tab), pl.BlockSpec((tm, LANES), tab)],
        out_specs=[pl.BlockSpec((tm, d), row), pl.BlockSpec((tm, rw_cols), row),
                   pl.BlockSpec((tm, da_w), row), pl.BlockSpec((tm, da_w), row),
                   qv_spec, pl.BlockSpec((tm, da_w), row), qv_spec],
        out_shape=[jax.ShapeDtypeStruct((n, d), f32), jax.ShapeDtypeStruct((n, rw_cols), f32),
                   jax.ShapeDtypeStruct((n, da_w), f32), jax.ShapeDtypeStruct((n, da_w), f32),
                   qv_shape, jax.ShapeDtypeStruct((n, da_w), bf16), qv_shape],
        compiler_params=_cparams("parallel"),
        name="ln_proj",
    )(x, g.reshape(1, d), b.reshape(1, d), w_bf, c, s1, s2)


def _heads(x, nh):
    return jnp.stack([x[:, h * RW_HD:(h + 1) * RW_HD] for h in range(nh)])


def _bmm(spec, a, b):
    return jnp.einsum(spec, a.astype(bf16), b.astype(bf16), preferred_element_type=f32)


def _rwkv_body(p_ref, shift_ref, s0_ref, mu_ref, w0_ref, w2_ref, a0_ref, a2_ref, g2_ref, kk_ref, ka_ref,
               rk_ref, lng_ref, lnb_ref,
               out_ref, sfin_ref,
               s_scr, prev_scr, r_scr, ld_scr, k_scr, v_scr, kk_scr, b_scr, y_scr, *, width, nh, tb):
    t_i = pl.program_id(1)
    nt = pl.num_programs(1)

    @pl.when(t_i == 0)
    def _():
        s_scr[...] = s0_ref[0]
        prev_scr[...] = shift_ref[0]

    p = p_ref[0]
    rows = lax.broadcasted_iota(jnp.int32, p.shape, 0)
    prev = jnp.where(rows == 0, prev_scr[...], pltpu.roll(p, 1, 0))
    prev_scr[...] = p[tb - 1:tb, :]
    pm = p + (prev - p) * mu_ref[...]
    W = width
    r = pm[:, :W]
    k = pm[:, W:2 * W]
    v = pm[:, 2 * W:3 * W]
    wa = pm[:, 3 * W:3 * W + RW_W_RANK + RW_A_RANK]
    gl = pm[:, 3 * W + RW_W_RANK + RW_A_RANK:]
    lw = jnp.dot(jnp.tanh(wa).astype(bf16), w2_ref[...], preferred_element_type=f32)
    la = jnp.dot(wa.astype(bf16), a2_ref[...], preferred_element_type=f32)
    w = -jax.nn.softplus(-(w0_ref[...] + lw)) - 0.5
    ld_scr[...] = -jnp.exp(w)
    a = jax.nn.sigmoid(a0_ref[...] + la)
    g = jnp.dot(jax.nn.sigmoid(gl).astype(bf16), g2_ref[...], preferred_element_type=f32)

    lane_h = lax.broadcasted_iota(jnp.int32, (W, W), 0) // RW_HD
    lane_h2 = lax.broadcasted_iota(jnp.int32, (W, W), 1) // RW_HD
    grp = (lane_h == lane_h2).astype(f32)

    kk = k * kk_ref[...]
    ss = _dot_exact_rhs(kk * kk, grp)
    kk = kk / jnp.maximum(jnp.sqrt(ss), 1e-12)
    k2 = k * (1.0 + (a - 1.0) * ka_ref[...])
    r_scr[...] = r
    k_scr[...] = k2
    v_scr[...] = v
    kk_scr[...] = kk
    b_scr[...] = kk * a
    bonus = _dot_exact_rhs(r * k2 * rk_ref[...], grp) * v

    L = CHUNK
    ri = lax.broadcasted_iota(jnp.int32, (L, L), 0)
    ci = lax.broadcasted_iota(jnp.int32, (L, L), 1)
    tri_incl = (ri >= ci).astype(f32)
    strict = (ri > ci)[None]
    incl = (ri >= ci)[None]
    eye = (ri == ci)[None]

    def chunk(c, carry):
        sl = pl.ds(pl.multiple_of(c * L, L), L)
        lw_c = ld_scr[sl, :]
        cin = _dot_exact_lhs(tri_incl, lw_c)
        cex = cin - lw_c
        ctot = cin[L - 1:L, :]
        e_in, e_ex, e_nin, e_rem = jnp.exp(cin), jnp.exp(cex), jnp.exp(-cin), jnp.exp(ctot - cin)
        kk_c, b_c, k_c, r_c, v_c = kk_scr[sl, :], b_scr[sl, :], k_scr[sl, :], r_scr[sl, :], v_scr[sl, :]
        al = _heads(kk_c * e_ex, nh)
        be = _heads(b_c * e_nin, nh)
        ka = _heads(k_c * e_nin, nh)
        rh = _heads(r_c * e_in, nh)
        bel = _heads(b_c * e_rem, nh)
        kal = _heads(k_c * e_rem, nh)
        vh = _heads(v_c, nh)
        wl = _heads(jnp.exp(ctot), nh)
        nt_ = 'hlk,hmk->hlm'
        nn_ = 'hlm,hmk->hlk'
        tn_ = 'hlk,hlv->hkv'
        mab = jnp.where(strict, _bmm(nt_, al, be), 0.0)
        mak = jnp.where(strict, _bmm(nt_, al, ka), 0.0)
        nrb = jnp.where(incl, _bmm(nt_, rh, be), 0.0)
        nrk = jnp.where(incl, _bmm(nt_, rh, ka), 0.0)
        n_ = -mab
        f_, q_ = n_, n_
        for _ in range(5):
            q_ = _bmm(nn_, q_, q_)
            f_ = f_ + q_ + _bmm(nn_, f_, q_)
        s0 = s_scr[...]
        x_ = _bmm(nt_, al, s0) + _bmm(nn_, mak, vh)
        pp = x_ + _bmm(nn_, f_, x_)
        y = _bmm(nt_, rh, s0) - _bmm(nn_, nrb, pp) + _bmm(nn_, nrk, vh)
        s_scr[...] = s0 * wl + _bmm(tn_, vh, kal) - _bmm(tn_, pp, bel)
        y_scr[sl, :] = jnp.concatenate([y[h] for h in range(nh)], axis=1)
        return carry

    lax.fori_loop(0, tb // L, chunk, 0)

    y = y_scr[...]
    inv_n = 1.0 / RW_HD
    ym = _dot_exact_rhs(y, grp) * inv_n
    yc = y - ym
    yv = _dot_exact_rhs(yc * yc, grp) * inv_n
    yn = yc * lax.rsqrt(yv + RW_GN_EPS) * lng_ref[...] + lnb_ref[...]
    out_ref[0] = (yn + bonus) * g

    @pl.when(t_i == nt - 1)
    def _():
        sfin_ref[0] = s_scr[...]


def _rwkv(p_rw, shift_prev, s0, mu, w0, w2, a0, a2, g2, k_k, k_a, r_k, lnx_g, lnx_b, *, block):
    bsz, t, cols = p_rw.shape
    nh = s0.shape[1]
    W = nh * RW_HD
    tb = block
    joint = RW_W_RANK + RW_A_RANK
    w2p = jnp.concatenate([w2, jnp.zeros((RW_A_RANK, W), w2.dtype)], 0).astype(bf16)
    a2p = jnp.concatenate([jnp.zeros((RW_W_RANK, W), a2.dtype), a2], 0).astype(bf16)
    vec = lambda x: x.reshape(1, -1)
    cst = lambda shape: pl.BlockSpec(shape, lambda b, i: (0,) * len(shape))
    body = functools.partial(_rwkv_body, width=W, nh=nh, tb=tb)
    big = lambda: pltpu.VMEM((tb, W), f32)
    return pl.pallas_call(
        body,
        grid=(bsz, t // tb),
        in_specs=[pl.BlockSpec((1, tb, cols), lambda b, i: (b, i, 0)),
                  pl.BlockSpec((1, 1, cols), lambda b, i: (b, 0, 0)),
                  pl.BlockSpec((1, nh, RW_HD, RW_HD), lambda b, i: (b, 0, 0, 0)),
                  cst((1, cols)), cst((1, W)), cst((joint, W)), cst((1, W)), cst((joint, W)),
                  cst((RW_G_RANK, W)), cst((1, W)), cst((1, W)), cst((1, W)), cst((1, W)), cst((1, W))],
        out_specs=[pl.BlockSpec((1, tb, W), lambda b, i: (b, i, 0)),
                   pl.BlockSpec((1, nh, RW_HD, RW_HD), lambda b, i: (b, 0, 0, 0))],
        out_shape=[jax.ShapeDtypeStruct((bsz, t, W), f32),
                   jax.ShapeDtypeStruct((bsz, nh, RW_HD, RW_HD), f32)],
        scratch_shapes=[pltpu.VMEM((nh, RW_HD, RW_HD), f32), pltpu.VMEM((1, cols), f32),
                        big(), big(), big(), big(), big(), big(), big()],
        compiler_params=_cparams("arbitrary", "arbitrary"),
        name="rwkv7",
    )(p_rw, shift_prev, s0, vec(mu), vec(w0), w2p, vec(a0), a2p, g2.astype(bf16), vec(k_k), vec(k_a),
      vec(r_k), vec(lnx_g), vec(lnx_b))


def _lam(lq1_ref, lk1_ref, lq2_ref, lk2_ref, lam_init):
    e1 = jnp.exp(jnp.sum(lq1_ref[...] * lk1_ref[...], axis=-1, keepdims=True))
    e2 = jnp.exp(jnp.sum(lq2_ref[...] * lk2_ref[...], axis=-1, keepdims=True))
    return e1 - e2 + lam_init


def _sub_q(q):
    lane = lax.broadcasted_iota(jnp.int32, q.shape, 1)
    zero = jnp.zeros_like(q)
    return jnp.where(lane < DA_HD, q, zero), jnp.where(lane >= DA_HD, q, zero)


def _scores(qm, k):
    return lax.dot_general(qm, k, (((1,), (1,)), ((), ())), preferred_element_type=f32)


def _finish(o1, o2, lam, g, lam_init):
    o = o1 - lam * o2
    return o * lax.rsqrt(jnp.mean(o * o, -1, keepdims=True) + LN_EPS) * g * (1.0 - lam_init)


def _attn_prompt_body(q_ref, k_ref, v_ref, lq1_ref, lk1_ref, lq2_ref, lk2_ref, g_ref, o_ref,
                      m_scr, l_scr, acc_scr, *, tq, tk, lam_init):
    i = pl.program_id(2)
    qt = q_ref[...]
    feat = lax.broadcasted_iota(jnp.int32, qt.shape, 0)
    zero = jnp.zeros_like(qt)
    qs = (jnp.where(feat < DA_HD, qt, zero), jnp.where(feat >= DA_HD, qt, zero))
    m_scr[...] = jnp.full(m_scr.shape, -jnp.inf, f32)
    l_scr[...] = jnp.zeros(l_scr.shape, f32)
    acc_scr[...] = jnp.zeros(acc_scr.shape, f32)

    def update(kb, vtb, mask):
        for c in range(2):
            s = jnp.dot(kb, qs[c], preferred_element_type=f32)
            if mask is not None:
                s = jnp.where(mask, s, -jnp.inf)
            m_old = m_scr[c]
            m_new = jnp.maximum(m_old, jnp.max(s, 0, keepdims=True))
            alpha = jnp.exp2(m_old - m_new)
            pexp = jnp.exp2(s - m_new)
            l_scr[c] = alpha * l_scr[c] + jnp.sum(pexp, 0, keepdims=True)
            acc_scr[c] = alpha * acc_scr[c] + jnp.dot(vtb, pexp.astype(bf16), preferred_element_type=f32)
            m_scr[c] = m_new

    n_full = (i * tq) // tk

    def blk(j, carry):
        sl = pl.ds(pl.multiple_of(j * tk, tk), tk)
        update(k_ref[sl, :], v_ref[:, sl], None)
        return carry

    lax.fori_loop(0, n_full, blk, 0)
    start = pl.multiple_of(n_full * tk, tk)
    sl = pl.ds(start, tk)
    kc = (start + lax.broadcasted_iota(jnp.int32, (tk, tq), 0)) // CHUNK
    qc = (i * tq + lax.broadcasted_iota(jnp.int32, (tk, tq), 1)) // CHUNK
    update(k_ref[sl, :], v_ref[:, sl], kc <= qc)
    lam = _lam(lq1_ref, lk1_ref, lq2_ref, lk2_ref, lam_init)
    o = acc_scr[0] / l_scr[0] - lam * (acc_scr[1] / l_scr[1])
    o = o * lax.rsqrt(jnp.mean(o * o, 0, keepdims=True) + LN_EPS)
    o_ref[...] = o.T * g_ref[...] * (1.0 - lam_init)


def _attn_prompt(qt, kb, vt, lq1, lk1, lq2, lk2, g, *, bsz, lam_init, block):
    n, da_w = kb.shape
    t = n // bsz
    nhead = da_w // DA_VD
    tq = block
    nq = t // tq
    vec = lambda x: x.reshape(1, -1)
    cst = lambda w: pl.BlockSpec((1, w), lambda b, h, i: (0, 0))
    body = functools.partial(_attn_prompt_body, tq=tq, tk=min(2 * tq, t), lam_init=lam_init)
    return pl.pallas_call(
        body,
        grid=(bsz, nhead, nq),
        in_specs=[pl.BlockSpec((DA_VD, tq), lambda b, h, i: (h, b * nq + i)),
                  pl.BlockSpec((t, DA_VD), lambda b, h, i: (b, h)),
                  pl.BlockSpec((DA_VD, t), lambda b, h, i: (h, b)),
                  cst(DA_HD), cst(DA_HD), cst(DA_HD), cst(DA_HD), cst(DA_VD)],
        out_specs=pl.BlockSpec((tq, DA_VD), lambda b, h, i: (b * nq + i, h)),
        out_shape=jax.ShapeDtypeStruct((n, da_w), f32),
        scratch_shapes=[pltpu.VMEM((2, 1, tq), f32), pltpu.VMEM((2, 1, tq), f32),
                        pltpu.VMEM((2, DA_VD, tq), f32)],
        compiler_params=_cparams("parallel", "parallel", "arbitrary"),
        name="diff_attn_prompt",
    )(qt, kb, vt, vec(lq1), vec(lk1), vec(lq2), vec(lk2), vec(g))


def _attn_sample_body(q_ref, kn_ref, vn_ref, ck_ref, cv_ref, lq1_ref, lk1_ref, lq2_ref, lk2_ref, g_ref, o_ref,
                      *, lam_init):
    qs = _sub_q(q_ref[...])
    ck = ck_ref[0].astype(bf16)
    cv = cv_ref[0].astype(bf16)
    kn = kn_ref[...]
    vn = vn_ref[...]
    outs = []
    for c in range(2):
        s_c = _scores(qs[c], ck)
        s_n = _scores(qs[c], kn)
        m = jnp.maximum(jnp.max(s_c, -1, keepdims=True), jnp.max(s_n, -1, keepdims=True))
        p_c = jnp.exp(s_c - m)
        p_n = jnp.exp(s_n - m)
        l = jnp.sum(p_c, -1, keepdims=True) + jnp.sum(p_n, -1, keepdims=True)
        o = (jnp.dot(p_c.astype(bf16), cv, preferred_element_type=f32)
             + jnp.dot(p_n.astype(bf16), vn, preferred_element_type=f32))
        outs.append(o / l)
    lam = _lam(lq1_ref, lk1_ref, lq2_ref, lk2_ref, lam_init)
    o_ref[...] = _finish(outs[0], outs[1], lam, g_ref[...], lam_init)


def _attn_sample(qb, kb, vb, cache_k, cache_v, lq1, lk1, lq2, lk2, g, *, lam_init):
    n, da_w = qb.shape
    bsz, past = cache_k.shape[0], cache_k.shape[1]
    t = n // bsz
    nhead = da_w // DA_VD
    ck = cache_k.reshape(bsz, past, da_w)
    cv = cache_v.reshape(bsz, past, da_w)
    vec = lambda x: x.reshape(1, -1)
    cst = lambda w: pl.BlockSpec((1, w), lambda b, h: (0, 0))
    new = pl.BlockSpec((t, DA_VD), lambda b, h: (b, h))
    old = pl.BlockSpec((1, past, DA_VD), lambda b, h: (b, 0, h))
    body = functools.partial(_attn_sample_body, lam_init=lam_init)
    return pl.pallas_call(
        body,
        grid=(bsz, nhead),
        in_specs=[new, new, new, old, old, cst(DA_HD), cst(DA_HD), cst(DA_HD), cst(DA_HD), cst(DA_VD)],
        out_specs=new,
        out_shape=jax.ShapeDtypeStruct((n, da_w), f32),
        compiler_params=_cparams("parallel", "parallel"),
        name="diff_attn_sample",
    )(qb, kb, vb, ck, cv, vec(lq1), vec(lk1), vec(lq2), vec(lk2), vec(g))


def _out_ln_body(h_ref, rw_ref, da_ref, w1_ref, w2_ref, g_ref, b_ref, o_ref, *, alpha):
    m = (jnp.dot(rw_ref[...].astype(bf16), w1_ref[...], preferred_element_type=f32)
         + jnp.dot(da_ref[...].astype(bf16), w2_ref[...], preferred_element_type=f32))
    o_ref[...] = _layer_norm(alpha * h_ref[...] + m, g_ref[...], b_ref[...])


def _out_ln(h, rw_out, da_out, w_out_bf, g, b, *, alpha, block):
    n, d = h.shape
    w_rw = rw_out.shape[1]
    w_da = da_out.shape[1]
    tm = block
    row = lambda i: (i, 0)
    full = lambda i: (0, 0)
    return pl.pallas_call(
        functools.partial(_out_ln_body, alpha=alpha),
        grid=(n // tm,),
        in_specs=[pl.BlockSpec((tm, d), row), pl.BlockSpec((tm, w_rw), row), pl.BlockSpec((tm, w_da), row),
                  pl.BlockSpec((w_rw, d), full), pl.BlockSpec((w_da, d), full),
                  pl.BlockSpec((1, d), full), pl.BlockSpec((1, d), full)],
        out_specs=pl.BlockSpec((tm, d), row),
        out_shape=jax.ShapeDtypeStruct((n, d), f32),
        compiler_params=_cparams("parallel"),
        name="out_proj_ln1",
    )(h, rw_out, da_out, w_out_bf[:w_rw], w_out_bf[w_rw:], g.reshape(1, d), b.reshape(1, d))


def _pair_list():
    return [(i, j) for i in range(PEER_TOPK) for j in range(PEER_TOPK) if (i + 1) * (j + 1) <= PEER_TOPK]


def _top_values(x, with_rank=False):
    out = []
    rank = jnp.full(x.shape, PEER_TOPK + 1.0, f32) if with_rank else None
    for j in range(PEER_TOPK):
        m = jnp.max(x, axis=0, keepdims=True)
        out.append(m)
        hit = x == m
        if with_rank:
            rank = jnp.where(hit, j + 1.0, rank)
        x = jnp.where(hit, -jnp.inf, x)
    return (out, rank) if with_rank else out


def _peer_body(h_ref, wq_ref, sk_ref, u_ref, vt_ref, g_ref, b_ref, o_ref,
               e1_scr, cnt_scr, r2_scr, e2_scr, acc_scr, *, alpha, a_per_blk):
    e = pl.program_id(1)
    ne = pl.num_programs(1)
    tb = h_ref.shape[0]
    nk = PEER_NKEYS

    @pl.when(e == 0)
    def _():
        hb = h_ref[...].astype(bf16)
        q = jnp.dot(hb, wq_ref[...], preferred_element_type=f32).astype(bf16)
        for h in range(PEER_HEADS):
            ss = []
            for c in range(2):
                col = (h * 2 + c) * PEER_HALF
                ss.append(lax.dot_general(sk_ref[h, c], q[:, col:col + PEER_HALF], (((1,), (1,)), ((), ())),
                                          preferred_element_type=f32))
            s1, s2 = ss
            t1 = _top_values(s1)
            t2, rank2 = _top_values(s2, with_rank=True)
            cands = [t1[i] + t2[j] for i, j in _pair_list()]
            pad = [jnp.full_like(cands[0], -jnp.inf)] * (-len(cands) % 8)
            tau = _top_values(jnp.concatenate(cands + pad, axis=0))[PEER_TOPK - 1]
            top = t1[0] + t2[0]
            z = jnp.zeros_like(tau)
            for cnd in cands:
                z = z + jnp.where(cnd >= tau, jnp.exp(cnd - top), 0.0)
            cnt = jnp.zeros_like(s1)
            for j in range(PEER_TOPK):
                cnt = cnt + jnp.where(s1 + t2[j] >= tau, 1.0, 0.0)
            e1_scr[h] = jnp.exp(s1 - t1[0]) / z
            cnt_scr[h] = cnt
            r2_scr[h] = rank2.astype(bf16)
            e2_scr[h] = jnp.exp(s2 - t2[0]).astype(bf16)
        acc_scr[...] = jnp.zeros(acc_scr.shape, f32)

    act = lax.dot_general(u_ref[...], h_ref[...].astype(bf16), (((1,), (1,)), ((), ())),
                          preferred_element_type=f32)
    gel = (0.5 * act * (1.0 + lax.erf(act * (1.0 / math.sqrt(2.0))))).astype(bf16)
    gates = []
    for al in range(a_per_blk):
        a = e * a_per_blk + al
        gsum = jnp.zeros((nk, tb), bf16)
        for h in range(PEER_HEADS):
            crow = cnt_scr[h, pl.ds(a, 1), :].astype(bf16)
            e1row = e1_scr[h, pl.ds(a, 1), :].astype(bf16)
            sel = r2_scr[h] <= crow
            gsum = gsum + jnp.where(sel, e2_scr[h], jnp.zeros_like(gsum)) * e1row
        gates.append(gsum)
    wgt = jnp.concatenate(gates, axis=0) * gel
    acc_scr[...] += jnp.dot(vt_ref[...], wgt, preferred_element_type=f32)

    @pl.when(e == ne - 1)
    def _():
        f = acc_scr[...].T
        o_ref[...] = _layer_norm(alpha * h_ref[...] + f, g_ref[...], b_ref[...])


def _peer(h, wq_bf, sk_bf, u_bf, vt_bf, g, b, *, alpha, block, a_per_blk):
    n, d = h.shape
    tb = block
    eb = a_per_blk * PEER_NKEYS
    n_exp = u_bf.shape[0]
    qcols = wq_bf.shape[1]
    body = functools.partial(_peer_body, alpha=alpha, a_per_blk=a_per_blk)
    keyed = lambda dt: pltpu.VMEM((PEER_HEADS, PEER_NKEYS, tb), dt)
    return pl.pallas_call(
        body,
        grid=(n // tb, n_exp // eb),
        in_specs=[pl.BlockSpec((tb, d), lambda i, e: (i, 0)),
                  pl.BlockSpec((d, qcols), lambda i, e: (0, 0)),
                  pl.BlockSpec((PEER_HEADS, 2, PEER_NKEYS, PEER_HALF), lambda i, e: (0, 0, 0, 0)),
                  pl.BlockSpec((eb, d), lambda i, e: (e, 0)),
                  pl.BlockSpec((d, eb), lambda i, e: (0, e)),
                  pl.BlockSpec((1, d), lambda i, e: (0, 0)), pl.BlockSpec((1, d), lambda i, e: (0, 0))],
        out_specs=pl.BlockSpec((tb, d), lambda i, e: (i, 0)),
        out_shape=jax.ShapeDtypeStruct((n, d), f32),
        scratch_shapes=[keyed(f32), keyed(f32), keyed(bf16), keyed(bf16), pltpu.VMEM((d, tb), f32)],
        compiler_params=_cparams("parallel", "arbitrary"),
        name="peer_ln2",
    )(h, wq_bf, sk_bf, u_bf, vt_bf, g.reshape(1, d), b.reshape(1, d))


def _pick(n, pref):
    b = pref
    while n % b:
        b //= 2
    return b


def kernel(x_prompt, x_sample, cache_k, cache_v, state_wkv, state_shift, ln_in_g, ln_in_b, w_in, shift_mu, rw_w0, rw_w2, rw_a0, rw_a2, rw_g2, rw_kk, rw_ka, rw_rk, rw_lnx_g, rw_lnx_b, da_lq1, da_lk1, da_lq2, da_lk2, da_subln_g, w_out, ln1_g, ln1_b, peer_wq, peer_subkeys, peer_u, peer_v, ln2_g, ln2_b):
    depth = w_in.shape[0]
    alpha = (2.0 * depth) ** 0.25
    bp, tp, d = x_prompt.shape
    bs, ts, _ = x_sample.shape
    past = cache_k.shape[2]
    nh_rw = state_wkv.shape[2]
    rw_w = nh_rw * RW_HD
    rw_cols = 3 * rw_w + RW_W_RANK + RW_A_RANK + RW_G_RANK
    da_w = DA_HEADS * DA_VD

    hp = x_prompt.reshape(bp * tp, d)
    hs = x_sample.reshape(bs * ts, d)
    pos_p = jnp.arange(tp)
    blk_s = _pick(bs * ts, 256)
    pos_s = past + (jnp.arange(max(blk_s, ts)) % ts)
    outs = [[] for _ in range(8)]
    for l in range(depth):
        lam_init = 0.8 - 0.6 * math.exp(-0.3 * l)
        w_in_bf = w_in[l].astype(bf16)
        w_out_bf = w_out[l].astype(bf16)
        wq_bf = peer_wq[l].astype(bf16)
        sk_bf = peer_subkeys[l].astype(bf16)
        u_bf = peer_u[l].astype(bf16)
        vt_bf = peer_v[l].astype(bf16).T

        def layer(h, bsz, t, pos, shift_prev, s0, ck, cv, blk):
            h, p_rw, k_rot, v, qb, kb, vb = _ln_proj(h, ln_in_g, ln_in_b, w_in_bf, pos, apply_ln=(l == 0),
                                                     rw_cols=rw_cols, block=blk, feature_major=ck is None)
            p_rw3 = p_rw.reshape(bsz, t, rw_cols)
            rw_out, s_fin = _rwkv(p_rw3, shift_prev, s0, shift_mu[l], rw_w0[l], rw_w2[l], rw_a0[l], rw_a2[l],
                                  rw_g2[l], rw_kk[l], rw_ka[l], rw_rk[l].reshape(-1), rw_lnx_g[l], rw_lnx_b[l],
                                  block=_pick(t, 256))
            if ck is None:
                da_out = _attn_prompt(qb, kb, vb, da_lq1[l], da_lk1[l], da_lq2[l], da_lk2[l], da_subln_g[l],
                                      bsz=bsz, lam_init=lam_init, block=_pick(t, 512))
            else:
                da_out = _attn_sample(qb, kb, vb, ck, cv, da_lq1[l], da_lk1[l], da_lq2[l], da_lk2[l],
                                      da_subln_g[l], lam_init=lam_init)
            h1 = _out_ln(h, rw_out.reshape(bsz * t, rw_w), da_out, w_out_bf, ln1_g[l], ln1_b[l],
                         alpha=alpha, block=blk)
            h2 = _peer(h1, wq_bf, sk_bf, u_bf, vt_bf, ln2_g[l], ln2_b[l], alpha=alpha,
                       block=_pick(bsz * t, 512), a_per_blk=8)
            return (h2, k_rot.reshape(bsz, t, DA_HEADS, DA_VD), v.reshape(bsz, t, DA_HEADS, DA_VD), s_fin,
                    p_rw3[:, t - 1:, :])

        shift0 = jnp.zeros((bp, 1, rw_cols), x_prompt.dtype)
        s_zero = jnp.zeros((bp, nh_rw, RW_HD, RW_HD), state_wkv.dtype)
        hp, kp, vp, sp, shp = layer(hp, bp, tp, pos_p, shift0, s_zero, None, None, _pick(bp * tp, 256))
        hs, ksn, vsn, ssn, shsn = layer(hs, bs, ts, pos_s, state_shift[l], state_wkv[l], cache_k[l], cache_v[l],
                                        blk_s)
        for lst, val in zip(outs, (kp, vp, sp, shp, ksn, vsn, ssn, shsn)):
            lst.append(val)
    st = [jnp.stack(o) for o in outs]
    return (hp.reshape(bp, tp, d), hs.reshape(bs, ts, d), *st)
```

```python
import functools
import math

import jax
import jax.numpy as jnp
import numpy as np
from jax import lax
from jax.experimental import pallas as pl
from jax.experimental.pallas import tpu as pltpu

f32 = jnp.float32
bf16 = jnp.bfloat16

CHUNK = 64
RW_HD = 64
RW_W_RANK = 64
RW_A_RANK = 64
RW_G_RANK = 128
RW_GN_EPS = 64e-5
DA_HEADS = 4
DA_HD = 64
DA_VD = 2 * DA_HD
ROT_DIM = DA_HD // 4
ROPE_THETA = 500000.0
DA_SCALE = 1.0 / math.sqrt(DA_HD)
LOG2E = math.log2(math.e)
PEER_HEADS = 8
PEER_NKEYS = 128
PEER_HALF = 128
PEER_TOPK = 16
LN_EPS = 1e-5

LANES = 128
VMEM_LIMIT = 56 * 1024 * 1024


def _cparams(*sem, flags=None):
    return pltpu.CompilerParams(dimension_semantics=sem, vmem_limit_bytes=VMEM_LIMIT, flags=flags)


def _layer_norm(x, g, b):
    mu = jnp.mean(x, -1, keepdims=True)
    xc = x - mu
    var = jnp.mean(xc * xc, -1, keepdims=True)
    return xc * lax.rsqrt(var + LN_EPS) * g + b


def _split3(x):
    hi = x.astype(bf16)
    r1 = x - hi.astype(f32)
    mid = r1.astype(bf16)
    lo = (r1 - mid.astype(f32)).astype(bf16)
    return hi, mid, lo


def _dot_exact_lhs(a01, x):
    a = a01.astype(bf16)
    hi, mid, lo = _split3(x)
    d = lambda y: jnp.dot(a, y, preferred_element_type=f32)
    return d(hi) + d(mid) + d(lo)


def _dot_exact_rhs(x, b01):
    b = b01.astype(bf16)
    hi, mid, lo = _split3(x)
    d = lambda y: jnp.dot(y, b, preferred_element_type=f32)
    return d(hi) + d(mid) + d(lo)


def _ln_proj_body(x_ref, g_ref, b_ref, w_ref, c_ref, s1_ref, s2_ref,
                  h_ref, prw_ref, k_ref, v_ref, qb_ref, kb_ref, vb_ref, *, apply_ln, rw_cols, da_w, feature_major):
    x = x_ref[...]
    h = _layer_norm(x, g_ref[...], b_ref[...]) if apply_ln else x
    h_ref[...] = h
    p = jnp.dot(h.astype(bf16), w_ref[...], preferred_element_type=f32)
    prw_ref[...] = p[:, :rw_cols]
    q = p[:, rw_cols:rw_cols + da_w]
    k = p[:, rw_cols + da_w:rw_cols + 2 * da_w]
    v = p[:, rw_cols + 2 * da_w:]
    rep = da_w // LANES
    c = jnp.concatenate([c_ref[...]] * rep, axis=1)
    s1 = jnp.concatenate([s1_ref[...]] * rep, axis=1)
    s2 = jnp.concatenate([s2_ref[...]] * rep, axis=1)
    half = ROT_DIM // 2

    def rope(t):
        return t * c + pltpu.roll(t, da_w - half, 1) * s1 + pltpu.roll(t, half, 1) * s2

    qr = rope(q)
    kr = rope(k)
    for hd in range(DA_HEADS):
        k_ref[:, hd, :] = kr[:, hd * DA_VD:(hd + 1) * DA_VD]
        v_ref[:, hd, :] = v[:, hd * DA_VD:(hd + 1) * DA_VD]
    kb_ref[...] = kr.astype(bf16)
    if feature_major:
        qb_ref[...] = (qr * (DA_SCALE * LOG2E)).T.astype(bf16)
        vb_ref[...] = v.T.astype(bf16)
    else:
        qb_ref[...] = (qr * DA_SCALE).astype(bf16)
        vb_ref[...] = v.astype(bf16)


def _rope_tables(pos):
    half = ROT_DIM // 2
    inv = np.power(ROPE_THETA, -np.arange(half, dtype=np.float64) * (2.0 / ROT_DIM))
    ang = np.asarray(pos, np.float64)[:, None] * inv[None, :]
    cos, sin = np.cos(ang), np.sin(ang)
    n = ang.shape[0]
    one = np.ones((n, DA_HD - ROT_DIM))
    zero = np.zeros((n, DA_HD - ROT_DIM))
    zh = np.zeros((n, half))
    c = np.concatenate([cos, cos, one], 1)
    s1 = np.concatenate([-sin, zh, zero], 1)
    s2 = np.concatenate([zh, sin, zero], 1)
    two = lambda t: jnp.asarray(np.concatenate([t, t], 1), f32)
    return two(c), two(s1), two(s2)


def _ln_proj(x, g, b, w_bf, pos, *, apply_ln, rw_cols, block, feature_major):
    n, d = x.shape
    cols = w_bf.shape[1]
    da_w = (cols - rw_cols) // 3
    tm = block
    ntab = pos.shape[0] // tm
    c, s1, s2 = _rope_tables(pos)
    row = lambda i: (i, 0)
    tab = lambda i: (i % ntab, 0)
    full = lambda i: (0, 0)
    body = functools.partial(_ln_proj_body, apply_ln=apply_ln, rw_cols=rw_cols, da_w=da_w,
                             feature_major=feature_major)
    if feature_major:
        qv_spec = pl.BlockSpec((da_w, tm), lambda i: (0, i))
        qv_shape = jax.ShapeDtypeStruct((da_w, n), bf16)
    else:
        qv_spec = pl.BlockSpec((tm, da_w), row)
        qv_shape = jax.ShapeDtypeStruct((n, da_w), bf16)
    return pl.pallas_call(
        body,
        grid=(n // tm,),
        in_specs=[pl.BlockSpec((tm, d), row), pl.BlockSpec((1, d), full), pl.BlockSpec((1, d), full),
                  pl.BlockSpec((d, cols), full),
                  pl.BlockSpec((tm, LANES), tab), pl.BlockSpec((tm, LANES), tab), pl.BlockSpec((tm, LANES), tab)],
        out_specs=[pl.BlockSpec((tm, d), row), pl.BlockSpec((tm, rw_cols), row),
                   pl.BlockSpec((tm, DA_HEADS, DA_VD), lambda i: (i, 0, 0)),
                   pl.BlockSpec((tm, DA_HEADS, DA_VD), lambda i: (i, 0, 0)),
                   qv_spec, pl.BlockSpec((tm, da_w), row), qv_spec],
        out_shape=[jax.ShapeDtypeStruct((n, d), f32), jax.ShapeDtypeStruct((n, rw_cols), f32),
                   jax.ShapeDtypeStruct((n, DA_HEADS, DA_VD), f32), jax.ShapeDtypeStruct((n, DA_HEADS, DA_VD), f32),
                   qv_shape, jax.ShapeDtypeStruct((n, da_w), bf16), qv_shape],
        compiler_params=_cparams("parallel"),
        name="ln_proj",
    )(x, g.reshape(1, d), b.reshape(1, d), w_bf, c, s1, s2)


def _heads(x, nh):
    return jnp.stack([x[:, h * RW_HD:(h + 1) * RW_HD] for h in range(nh)])


def _bmm(spec, a, b):
    return jnp.einsum(spec, a.astype(bf16), b.astype(bf16), preferred_element_type=f32)


def _rwkv_body(p_ref, shift_ref, s0_ref, mu_ref, w0_ref, w2_ref, a0_ref, a2_ref, g2_ref, kk_ref, ka_ref,
               rk_ref, lng_ref, lnb_ref,
               out_ref, sfin_ref,
               s_scr, prev_scr, r_scr, ld_scr, k_scr, v_scr, kk_scr, b_scr, y_scr, g_scr, bonus_scr,
               *, width, nh, tb, nb):
    t_i = pl.program_id(1)
    nt = pl.num_programs(1)
    W = width

    @pl.when(t_i == 0)
    def _():
        for s in range(nb):
            s_scr[s * nh:(s + 1) * nh] = s0_ref[s]
        prev_scr[...] = shift_ref[...]

    lane_h = lax.broadcasted_iota(jnp.int32, (W, W), 0) // RW_HD
    lane_h2 = lax.broadcasted_iota(jnp.int32, (W, W), 1) // RW_HD
    grp = (lane_h == lane_h2).astype(f32)

    for s in range(nb):
        p = p_ref[s]
        rows = lax.broadcasted_iota(jnp.int32, p.shape, 0)
        prev = jnp.where(rows == 0, prev_scr[s], pltpu.roll(p, 1, 0))
        prev_scr[s] = p[tb - 1:tb, :]
        pm = p + (prev - p) * mu_ref[...]
        r = pm[:, :W]
        k = pm[:, W:2 * W]
        v = pm[:, 2 * W:3 * W]
        wa = pm[:, 3 * W:3 * W + RW_W_RANK + RW_A_RANK]
        gl = pm[:, 3 * W + RW_W_RANK + RW_A_RANK:]
        lw = jnp.dot(jnp.tanh(wa).astype(bf16), w2_ref[...], preferred_element_type=f32)
        la = jnp.dot(wa.astype(bf16), a2_ref[...], preferred_element_type=f32)
        w = -jax.nn.softplus(-(w0_ref[...] + lw)) - 0.5
        ld_scr[s] = -jnp.exp(w)
        a = jax.nn.sigmoid(a0_ref[...] + la)
        g_scr[s] = jnp.dot(jax.nn.sigmoid(gl).astype(bf16), g2_ref[...], preferred_element_type=f32)
        kk = k * kk_ref[...]
        ss = _dot_exact_rhs(kk * kk, grp)
        kk = kk / jnp.maximum(jnp.sqrt(ss), 1e-12)
        k2 = k * (1.0 + (a - 1.0) * ka_ref[...])
        r_scr[s] = r
        k_scr[s] = k2
        v_scr[s] = v
        kk_scr[s] = kk
        b_scr[s] = kk * a
        bonus_scr[s] = _dot_exact_rhs(r * k2 * rk_ref[...], grp) * v

    L = CHUNK
    ri = lax.broadcasted_iota(jnp.int32, (L, L), 0)
    ci = lax.broadcasted_iota(jnp.int32, (L, L), 1)
    tri_incl = (ri >= ci).astype(f32)
    strict = (ri > ci)[None]
    incl = (ri >= ci)[None]
    names = ("al", "be", "ka", "rh", "bel", "kal", "vh", "wl")

    def chunk(c, carry):
        sl = pl.ds(pl.multiple_of(c * L, L), L)
        parts = {name: [] for name in names}
        for s in range(nb):
            lw_c = ld_scr[s, sl, :]
            cin = _dot_exact_lhs(tri_incl, lw_c)
            cex = cin - lw_c
            ctot = cin[L - 1:L, :]
            e_in, e_ex, e_nin, e_rem = jnp.exp(cin), jnp.exp(cex), jnp.exp(-cin), jnp.exp(ctot - cin)
            kk_c, b_c, k_c = kk_scr[s, sl, :], b_scr[s, sl, :], k_scr[s, sl, :]
            parts["al"].append(_heads(kk_c * e_ex, nh))
            parts["be"].append(_heads(b_c * e_nin, nh))
            parts["ka"].append(_heads(k_c * e_nin, nh))
            parts["rh"].append(_heads(r_scr[s, sl, :] * e_in, nh))
            parts["bel"].append(_heads(b_c * e_rem, nh))
            parts["kal"].append(_heads(k_c * e_rem, nh))
            parts["vh"].append(_heads(v_scr[s, sl, :], nh))
            parts["wl"].append(_heads(jnp.exp(ctot), nh))
        al, be, ka, rh, bel, kal, vh, wl = (jnp.concatenate(parts[name], axis=0) for name in names)
        nt_ = 'hlk,hmk->hlm'
        nn_ = 'hlm,hmk->hlk'
        tn_ = 'hlk,hlv->hkv'
        mab = jnp.where(strict, _bmm(nt_, al, be), 0.0)
        mak = jnp.where(strict, _bmm(nt_, al, ka), 0.0)
        nrb = jnp.where(incl, _bmm(nt_, rh, be), 0.0)
        nrk = jnp.where(incl, _bmm(nt_, rh, ka), 0.0)
        n_ = -mab
        f_, q_ = n_, n_
        for _ in range(5):
            q_ = _bmm(nn_, q_, q_)
            f_ = f_ + q_ + _bmm(nn_, f_, q_)
        s0 = s_scr[...]
        x_ = _bmm(nt_, al, s0) + _bmm(nn_, mak, vh)
        pp = x_ + _bmm(nn_, f_, x_)
        y = _bmm(nt_, rh, s0) - _bmm(nn_, nrb, pp) + _bmm(nn_, nrk, vh)
        s_scr[...] = s0 * wl + _bmm(tn_, vh, kal) - _bmm(tn_, pp, bel)
        for s in range(nb):
            y_scr[s, sl, :] = jnp.concatenate([y[s * nh + h] for h in range(nh)], axis=1)
        return carry

    lax.fori_loop(0, tb // L, chunk, 0)

    inv_n = 1.0 / RW_HD
    for s in range(nb):
        y = y_scr[s]
        ym = _dot_exact_rhs(y, grp) * inv_n
        yc = y - ym
        yv = _dot_exact_rhs(yc * yc, grp) * inv_n
        yn = yc * lax.rsqrt(yv + RW_GN_EPS) * lng_ref[...] + lnb_ref[...]
        out_ref[s] = (yn + bonus_scr[s]) * g_scr[s]

    @pl.when(t_i == nt - 1)
    def _():
        for s in range(nb):
            sfin_ref[s] = s_scr[s * nh:(s + 1) * nh]


def _rwkv(p_rw, shift_prev, s0, mu, w0, w2, a0, a2, g2, k_k, k_a, r_k, lnx_g, lnx_b, *, block, streams):
    bsz, t, cols = p_rw.shape
    nh = s0.shape[1]
    W = nh * RW_HD
    tb = block
    nb = streams
    joint = RW_W_RANK + RW_A_RANK
    w2p = jnp.concatenate([w2, jnp.zeros((RW_A_RANK, W), w2.dtype)], 0).astype(bf16)
    a2p = jnp.concatenate([jnp.zeros((RW_W_RANK, W), a2.dtype), a2], 0).astype(bf16)
    vec = lambda x: x.reshape(1, -1)
    cst = lambda shape: pl.BlockSpec(shape, lambda b, i: (0,) * len(shape))
    body = functools.partial(_rwkv_body, width=W, nh=nh, tb=tb, nb=nb)
    big = lambda: pltpu.VMEM((nb, tb, W), f32)
    return pl.pallas_call(
        body,
        grid=(bsz // nb, t // tb),
        in_specs=[pl.BlockSpec((nb, tb, cols), lambda b, i: (b, i, 0)),
                  pl.BlockSpec((nb, 1, cols), lambda b, i: (b, 0, 0)),
                  pl.BlockSpec((nb, nh, RW_HD, RW_HD), lambda b, i: (b, 0, 0, 0)),
                  cst((1, cols)), cst((1, W)), cst((joint, W)), cst((1, W)), cst((joint, W)),
                  cst((RW_G_RANK, W)), cst((1, W)), cst((1, W)), cst((1, W)), cst((1, W)), cst((1, W))],
        out_specs=[pl.BlockSpec((nb, tb, W), lambda b, i: (b, i, 0)),
                   pl.BlockSpec((nb, nh, RW_HD, RW_HD), lambda b, i: (b, 0, 0, 0))],
        out_shape=[jax.ShapeDtypeStruct((bsz, t, W), f32),
                   jax.ShapeDtypeStruct((bsz, nh, RW_HD, RW_HD), f32)],
        scratch_shapes=[pltpu.VMEM((nb * nh, RW_HD, RW_HD), f32), pltpu.VMEM((nb, 1, cols), f32),
                        big(), big(), big(), big(), big(), big(), big(), big(), big()],
        compiler_params=_cparams("arbitrary", "arbitrary"),
        name="rwkv7",
    )(p_rw, shift_prev, s0, vec(mu), vec(w0), w2p, vec(a0), a2p, g2.astype(bf16), vec(k_k), vec(k_a),
      vec(r_k), vec(lnx_g), vec(lnx_b))


def _lam(lq1_ref, lk1_ref, lq2_ref, lk2_ref, lam_init):
    e1 = jnp.exp(jnp.sum(lq1_ref[...] * lk1_ref[...], axis=-1, keepdims=True))
    e2 = jnp.exp(jnp.sum(lq2_ref[...] * lk2_ref[...], axis=-1, keepdims=True))
    return e1 - e2 + lam_init


def _sub_q(q):
    lane = lax.broadcasted_iota(jnp.int32, q.shape, 1)
    zero = jnp.zeros_like(q)
    return jnp.where(lane < DA_HD, q, zero), jnp.where(lane >= DA_HD, q, zero)


def _scores(qm, k):
    return lax.dot_general(qm, k, (((1,), (1,)), ((), ())), preferred_element_type=f32)


def _finish(o1, o2, lam, g, lam_init):
    o = o1 - lam * o2
    return o * lax.rsqrt(jnp.mean(o * o, -1, keepdims=True) + LN_EPS) * g * (1.0 - lam_init)


def _attn_prompt_body(q_ref, k_ref, v_ref, lq1_ref, lk1_ref, lq2_ref, lk2_ref, g_ref, o_ref,
                      m_scr, l_scr, acc_scr, *, tq, tk, lam_init):
    i = pl.program_id(2)
    qt = q_ref[...]
    feat = lax.broadcasted_iota(jnp.int32, qt.shape, 0)
    zero = jnp.zeros_like(qt)
    qs = (jnp.where(feat < DA_HD, qt, zero), jnp.where(feat >= DA_HD, qt, zero))
    m_scr[...] = jnp.full(m_scr.shape, -jnp.inf, f32)
    l_scr[...] = jnp.zeros(l_scr.shape, f32)
    acc_scr[...] = jnp.zeros(acc_scr.shape, f32)

    def update(kb, vtb, mask):
        for c in range(2):
            s = jnp.dot(kb, qs[c], preferred_element_type=f32)
            if mask is not None:
                s = jnp.where(mask, s, -jnp.inf)
            m_old = m_scr[c]
            m_new = jnp.maximum(m_old, jnp.max(s, 0, keepdims=True))
            alpha = jnp.exp2(m_old - m_new)
            pexp = jnp.exp2(s - m_new)
            l_scr[c] = alpha * l_scr[c] + jnp.sum(pexp, 0, keepdims=True)
            acc_scr[c] = alpha * acc_scr[c] + jnp.dot(vtb, pexp.astype(bf16), preferred_element_type=f32)
            m_scr[c] = m_new

    n_full = (i * tq) // tk

    def blk(j, carry):
        sl = pl.ds(pl.multiple_of(j * tk, tk), tk)
        update(k_ref[sl, :], v_ref[:, sl], None)
        return carry

    lax.fori_loop(0, n_full, blk, 0)
    start = pl.multiple_of(n_full * tk, tk)
    sl = pl.ds(start, tk)
    kc = (start + lax.broadcasted_iota(jnp.int32, (tk, tq), 0)) // CHUNK
    qc = (i * tq + lax.broadcasted_iota(jnp.int32, (tk, tq), 1)) // CHUNK
    update(k_ref[sl, :], v_ref[:, sl], kc <= qc)
    lam = _lam(lq1_ref, lk1_ref, lq2_ref, lk2_ref, lam_init)
    o = acc_scr[0] / l_scr[0] - lam * (acc_scr[1] / l_scr[1])
    o = o * lax.rsqrt(jnp.mean(o * o, 0, keepdims=True) + LN_EPS)
    o_ref[...] = o.T * g_ref[...] * (1.0 - lam_init)


def _attn_prompt(qt, kb, vt, lq1, lk1, lq2, lk2, g, *, bsz, lam_init, block):
    n, da_w = kb.shape
    t = n // bsz
    nhead = da_w // DA_VD
    tq = block
    nq = t // tq
    vec = lambda x: x.reshape(1, -1)
    cst = lambda w: pl.BlockSpec((1, w), lambda b, h, i: (0, 0))
    body = functools.partial(_attn_prompt_body, tq=tq, tk=min(2 * tq, t), lam_init=lam_init)
    return pl.pallas_call(
        body,
        grid=(bsz, nhead, nq),
        in_specs=[pl.BlockSpec((DA_VD, tq), lambda b, h, i: (h, b * nq + i)),
                  pl.BlockSpec((t, DA_VD), lambda b, h, i: (b, h)),
                  pl.BlockSpec((DA_VD, t), lambda b, h, i: (h, b)),
                  cst(DA_HD), cst(DA_HD), cst(DA_HD), cst(DA_HD), cst(DA_VD)],
        out_specs=pl.BlockSpec((tq, DA_VD), lambda b, h, i: (b * nq + i, h)),
        out_shape=jax.ShapeDtypeStruct((n, da_w), f32),
        scratch_shapes=[pltpu.VMEM((2, 1, tq), f32), pltpu.VMEM((2, 1, tq), f32),
                        pltpu.VMEM((2, DA_VD, tq), f32)],
        compiler_params=_cparams("parallel", "parallel", "arbitrary"),
        name="diff_attn_prompt",
    )(qt, kb, vt, vec(lq1), vec(lk1), vec(lq2), vec(lk2), vec(g))


def _attn_sample_body(q_ref, kn_ref, vn_ref, ck_ref, cv_ref, lq1_ref, lk1_ref, lq2_ref, lk2_ref, g_ref, o_ref,
                      m_scr, l_scr, acc_scr, *, nhead, lam_init):
    j = pl.program_id(1)
    nj = pl.num_programs(1)
    t = q_ref.shape[0]

    @pl.when(j == 0)
    def _():
        m_scr[...] = jnp.full(m_scr.shape, -jnp.inf, f32)
        l_scr[...] = jnp.zeros(l_scr.shape, f32)
        acc_scr[...] = jnp.zeros(acc_scr.shape, f32)

    def update(h, k, v):
        q2 = jnp.concatenate(_sub_q(q_ref[:, h * DA_VD:(h + 1) * DA_VD]), axis=0)
        s = _scores(q2, k)
        m_old = m_scr[h]
        m_new = jnp.maximum(m_old, jnp.max(s, -1, keepdims=True))
        alpha = jnp.exp(m_old - m_new)
        pexp = jnp.exp(s - m_new)
        l_scr[h] = alpha * l_scr[h] + jnp.sum(pexp, -1, keepdims=True)
        acc_scr[h] = alpha * acc_scr[h] + jnp.dot(pexp.astype(bf16), v, preferred_element_type=f32)
        m_scr[h] = m_new

    for h in range(nhead):
        update(h, ck_ref[0, :, h, :].astype(bf16), cv_ref[0, :, h, :].astype(bf16))

    @pl.when(j == nj - 1)
    def _():
        lam = _lam(lq1_ref, lk1_ref, lq2_ref, lk2_ref, lam_init)
        for h in range(nhead):
            cols = slice(h * DA_VD, (h + 1) * DA_VD)
            update(h, kn_ref[:, cols], vn_ref[:, cols])
            o = acc_scr[h] / l_scr[h]
            o_ref[:, cols] = _finish(o[:t], o[t:], lam, g_ref[...], lam_init)


def _attn_sample(qb, kb, vb, cache_k, cache_v, lq1, lk1, lq2, lk2, g, *, lam_init, block):
    n, da_w = qb.shape
    bsz, past, nhead, _ = cache_k.shape
    t = n // bsz
    pb = block
    vec = lambda x: x.reshape(1, -1)
    cst = lambda w: pl.BlockSpec((1, w), lambda b, j: (0, 0))
    new = pl.BlockSpec((t, da_w), lambda b, j: (b, 0))
    old = pl.BlockSpec((1, pb, nhead, DA_VD), lambda b, j: (b, j, 0, 0))
    body = functools.partial(_attn_sample_body, nhead=nhead, lam_init=lam_init)
    return pl.pallas_call(
        body,
        grid=(bsz, past // pb),
        in_specs=[new, new, new, old, old, cst(DA_HD), cst(DA_HD), cst(DA_HD), cst(DA_HD), cst(DA_VD)],
        out_specs=new,
        out_shape=jax.ShapeDtypeStruct((n, da_w), f32),
        scratch_shapes=[pltpu.VMEM((nhead, 2 * t, 1), f32), pltpu.VMEM((nhead, 2 * t, 1), f32),
                        pltpu.VMEM((nhead, 2 * t, DA_VD), f32)],
        compiler_params=_cparams("parallel", "arbitrary"),
        name="diff_attn_sample",
    )(qb, kb, vb, cache_k, cache_v, vec(lq1), vec(lk1), vec(lq2), vec(lk2), vec(g))


def _out_ln_body(h_ref, rw_ref, da_ref, w1_ref, w2_ref, g_ref, b_ref, o_ref, *, alpha):
    m = (jnp.dot(rw_ref[...].astype(bf16), w1_ref[...], preferred_element_type=f32)
         + jnp.dot(da_ref[...].astype(bf16), w2_ref[...], preferred_element_type=f32))
    o_ref[...] = _layer_norm(alpha * h_ref[...] + m, g_ref[...], b_ref[...])


def _out_ln(h, rw_out, da_out, w_out_bf, g, b, *, alpha, block):
    n, d = h.shape
    w_rw = rw_out.shape[1]
    w_da = da_out.shape[1]
    tm = block
    row = lambda i: (i, 0)
    full = lambda i: (0, 0)
    return pl.pallas_call(
        functools.partial(_out_ln_body, alpha=alpha),
        grid=(n // tm,),
        in_specs=[pl.BlockSpec((tm, d), row), pl.BlockSpec((tm, w_rw), row), pl.BlockSpec((tm, w_da), row),
                  pl.BlockSpec((w_rw, d), full), pl.BlockSpec((w_da, d), full),
                  pl.BlockSpec((1, d), full), pl.BlockSpec((1, d), full)],
        out_specs=pl.BlockSpec((tm, d), row),
        out_shape=jax.ShapeDtypeStruct((n, d), f32),
        compiler_params=_cparams("parallel"),
        name="out_proj_ln1",
    )(h, rw_out, da_out, w_out_bf[:w_rw], w_out_bf[w_rw:], g.reshape(1, d), b.reshape(1, d))


def _pair_list():
    return [(i, j) for i in range(PEER_TOPK) for j in range(PEER_TOPK) if (i + 1) * (j + 1) <= PEER_TOPK]


def _top_values(x, with_rank=False):
    out = []
    rank = jnp.full(x.shape, PEER_TOPK + 1.0, f32) if with_rank else None
    for j in range(PEER_TOPK):
        m = jnp.max(x, axis=0, keepdims=True)
        out.append(m)
        hit = x == m
        if with_rank:
            rank = jnp.where(hit, j + 1.0, rank)
        x = jnp.where(hit, -jnp.inf, x)
    return (out, rank) if with_rank else out


def _peer_body(h_ref, wq_ref, sk_ref, u_ref, vt_ref, g_ref, b_ref, o_ref,
               e1_scr, cnt_scr, r2_scr, e2_scr, acc_scr, hbt_scr, *, alpha, a_per_blk):
    e = pl.program_id(1)
    ne = pl.num_programs(1)
    tb = h_ref.shape[0]
    nk = PEER_NKEYS

    @pl.when(e == 0)
    def _():
        hb = h_ref[...].astype(bf16)
        hbt_scr[...] = h_ref[...].T.astype(bf16)
        q =jnp.dot(hb, wq_ref[...], preferred_element_type=f32).astype(bf16)
        for h in range(PEER_HEADS):
            ss = []
            for c in range(2):
                col = (h * 2 + c) * PEER_HALF
                ss.append(lax.dot_general(sk_ref[h, c], q[:, col:col + PEER_HALF], (((1,), (1,)), ((), ())),
                                          preferred_element_type=f32))
            for lt in range(tb // LANES):
                cols = slice(lt * LANES, (lt + 1) * LANES)
                s1, s2 = ss[0][:, cols], ss[1][:, cols]
                t1 = _top_values(s1)
                t2, rank2 = _top_values(s2, with_rank=True)
                cands = [t1[i] + t2[j] for i, j in _pair_list()]
                pad = [jnp.full_like(cands[0], -jnp.inf)] * (-len(cands) % 8)
                tau = _top_values(jnp.concatenate(cands + pad, axis=0))[PEER_TOPK - 1]
                top = t1[0] + t2[0]
                z = jnp.zeros_like(tau)
                for cnd in cands:
                    z = z + jnp.where(cnd >= tau, jnp.exp(cnd - top), 0.0)
                t1_all = jnp.concatenate(t1, axis=0)
                c_all = jnp.zeros_like(t1_all)
                for j in range(PEER_TOPK):
                    c_all = c_all + jnp.where(t1_all + t2[j] >= tau, 1.0, 0.0)
                cnt = jnp.zeros_like(s1)
                for i in range(PEER_TOPK):
                    cnt = jnp.where(s1 == t1[i], c_all[i:i + 1, :], cnt)
                e1_scr[h, :, cols] = jnp.exp(s1 - t1[0]) / z
                cnt_scr[h, :, cols] = cnt
                r2_scr[h, :, cols] = rank2.astype(bf16)
                e2_scr[h, :, cols] = jnp.exp(s2 - t2[0]).astype(bf16)
        acc_scr[...] = jnp.zeros(acc_scr.shape, f32)

    act = jnp.dot(u_ref[...], hbt_scr[...], preferred_element_type=f32)
    gel = (0.5 * act * (1.0 + lax.erf(act * (1.0 / math.sqrt(2.0))))).astype(bf16)
    blk_rows = pl.ds(pl.multiple_of(e * a_per_blk, a_per_blk), a_per_blk)
    cnt_rows = cnt_scr[:, blk_rows, :].astype(bf16)
    e1_rows = e1_scr[:, blk_rows, :].astype(bf16)
    gates = []
    for al in range(a_per_blk):
        gsum = jnp.zeros((nk, tb), bf16)
        for h in range(PEER_HEADS):
            sel = r2_scr[h] <= cnt_rows[h, al:al + 1, :]
            gsum = gsum + jnp.where(sel, e2_scr[h], jnp.zeros_like(gsum)) * e1_rows[h, al:al + 1, :]
        gates.append(gsum)
    wgt = jnp.concatenate(gates, axis=0) * gel
    acc_scr[...] += jnp.dot(vt_ref[...], wgt, preferred_element_type=f32)

    @pl.when(e == ne - 1)
    def _():
        f = acc_scr[...].T
        o_ref[...] = _layer_norm(alpha * h_ref[...] + f, g_ref[...], b_ref[...])


def _peer(h, wq_bf, sk_bf, u_bf, vt_bf, g, b, *, alpha, block, a_per_blk):
    n, d = h.shape
    tb = block
    eb = a_per_blk * PEER_NKEYS
    ne = u_bf.shape[0] // eb
    qcols = wq_bf.shape[1]
    body = functools.partial(_peer_body, alpha=alpha, a_per_blk=a_per_blk)
    keyed = lambda dt: pltpu.VMEM((PEER_HEADS, PEER_NKEYS, tb), dt)
    return pl.pallas_call(
        body,
        grid=(n // tb, ne),
        in_specs=[pl.BlockSpec((tb, d), lambda i, e: (i, 0)),
                  pl.BlockSpec((d, qcols), lambda i, e: (0, 0)),
                  pl.BlockSpec((PEER_HEADS, 2, PEER_NKEYS, PEER_HALF), lambda i, e: (0, 0, 0, 0)),
                  pl.BlockSpec((eb, d), lambda i, e: (e, 0)),
                  pl.BlockSpec((d, eb), lambda i, e: (0, e)),
                  pl.BlockSpec((1, d), lambda i, e: (0, 0)), pl.BlockSpec((1, d), lambda i, e: (0, 0))],
        out_specs=pl.BlockSpec((tb, d), lambda i, e: (i, 0)),
        out_shape=jax.ShapeDtypeStruct((n, d), f32),
        scratch_shapes=[keyed(f32), keyed(f32), keyed(bf16), keyed(bf16), pltpu.VMEM((d, tb), f32),
                        pltpu.VMEM((d, tb), bf16)],
        compiler_params=_cparams("parallel", "arbitrary"),
        name="peer_ln2",
    )(h, wq_bf, sk_bf, u_bf, vt_bf, g.reshape(1, d), b.reshape(1, d))


def _pick(n, pref):
    b = pref
    while n % b:
        b //= 2
    return b


def kernel(x_prompt, x_sample, cache_k, cache_v, state_wkv, state_shift, ln_in_g, ln_in_b, w_in, shift_mu, rw_w0, rw_w2, rw_a0, rw_a2, rw_g2, rw_kk, rw_ka, rw_rk, rw_lnx_g, rw_lnx_b, da_lq1, da_lk1, da_lq2, da_lk2, da_subln_g, w_out, ln1_g, ln1_b, peer_wq, peer_subkeys, peer_u, peer_v, ln2_g, ln2_b):
    depth = w_in.shape[0]
    alpha = (2.0 * depth) ** 0.25
    bp, tp, d = x_prompt.shape
    bs, ts, _ = x_sample.shape
    past = cache_k.shape[2]
    nh_rw = state_wkv.shape[2]
    rw_w = nh_rw * RW_HD
    rw_cols = 3 * rw_w + RW_W_RANK + RW_A_RANK + RW_G_RANK
    da_w = DA_HEADS * DA_VD

    hp = x_prompt.reshape(bp * tp, d)
    hs = x_sample.reshape(bs * ts, d)
    pos_p = np.arange(tp)
    blk_s = _pick(bs * ts, 256)
    pos_s = past + (np.arange(max(blk_s, ts)) % ts)
    outs = [[] for _ in range(8)]
    for l in range(depth):
        lam_init = 0.8 - 0.6 * math.exp(-0.3 * l)
        w_in_bf = w_in[l].astype(bf16)
        w_out_bf = w_out[l].astype(bf16)
        wq_bf = peer_wq[l].astype(bf16)
        sk_bf = peer_subkeys[l].astype(bf16)
        u_bf = peer_u[l].astype(bf16)
        vt_bf = peer_v[l].astype(bf16).T

        def layer(h, bsz, t, pos, shift_prev, s0, ck, cv, blk):
            h, p_rw, k_rot, v, qb, kb, vb = _ln_proj(h, ln_in_g, ln_in_b, w_in_bf, pos, apply_ln=(l == 0),
                                                     rw_cols=rw_cols, block=blk, feature_major=ck is None)
            p_rw3 = p_rw.reshape(bsz, t, rw_cols)
            rw_out, s_fin = _rwkv(p_rw3, shift_prev, s0, shift_mu[l], rw_w0[l], rw_w2[l], rw_a0[l], rw_a2[l],
                                  rw_g2[l], rw_kk[l], rw_ka[l], rw_rk[l].reshape(-1), rw_lnx_g[l], rw_lnx_b[l],
                                  block=_pick(t, 256), streams=_pick(bsz, 2 if ck is None else 4))
            if ck is None:
                da_out = _attn_prompt(qb, kb, vb, da_lq1[l], da_lk1[l], da_lq2[l], da_lk2[l], da_subln_g[l],
                                      bsz=bsz, lam_init=lam_init, block=_pick(t, 512))
            else:
                da_out = _attn_sample(qb, kb, vb, ck, cv, da_lq1[l], da_lk1[l], da_lq2[l], da_lk2[l],
                                      da_subln_g[l], lam_init=lam_init, block=_pick(ck.shape[1], 1024))
            h1 = _out_ln(h, rw_out.reshape(bsz * t, rw_w), da_out, w_out_bf, ln1_g[l], ln1_b[l],
                         alpha=alpha, block=blk)
            h2 = _peer(h1, wq_bf, sk_bf, u_bf, vt_bf, ln2_g[l], ln2_b[l], alpha=alpha,
                       block=_pick(bsz * t, 512), a_per_blk=8)
            return (h2, k_rot.reshape(bsz, t, DA_HEADS, DA_VD), v.reshape(bsz, t, DA_HEADS, DA_VD), s_fin,
                    p_rw3[:, t - 1:, :])

        shift0 = jnp.zeros((bp, 1, rw_cols), x_prompt.dtype)
        s_zero = jnp.zeros((bp, nh_rw, RW_HD, RW_HD), state_wkv.dtype)
        hp, kp, vp, sp, shp = layer(hp, bp, tp, pos_p, shift0, s_zero, None, None, _pick(bp * tp, 256))
        hs, ksn, vsn, ssn, shsn = layer(hs, bs, ts, pos_s, state_shift[l], state_wkv[l], cache_k[l], cache_v[l],
                                        blk_s)
        for lst, val in zip(outs, (kp, vp, sp, shp, ksn, vsn, ssn, shsn)):
            lst.append(val)
    st = [o[0][None] if depth == 1 else jnp.stack(o) for o in outs]
    return (hp.reshape(bp, tp, d), hs.reshape(bs, ts, d), *st)
```
